```python
import math
import jax, jax.numpy as jnp
from jax import lax
import numpy as np

D_MODEL = 2048
BATCH = 1
SEQ = 8192
DEPTH = 4

D_FF = 5632
MLA_HEADS = 16
MLA_Q_RANK = 512
MLA_KV_RANK = 512
MLA_NOPE = 128
MLA_ROPE = 64
MLA_QK = MLA_NOPE + MLA_ROPE
MLA_V = 128
ROPE_THETA = 10000.0
Q_BLOCK = 128
SWA_Q_HEADS = 32
SWA_KV_HEADS = 8
SWA_GROUP = SWA_Q_HEADS // SWA_KV_HEADS
SWA_HEAD_DIM = 64
WINDOW = 128
BAND_BLOCK = 128
SWA_QKV = (SWA_Q_HEADS + 2 * SWA_KV_HEADS) * SWA_HEAD_DIM
N_MIXERS = 2
N_MLA_LAYERS = (DEPTH + 1) // 2
N_SWA_LAYERS = DEPTH // 2
EPS = 1e-6
NEG_INF = -1e30

kernel_name = "hybrid_mla_swa_macaron_encoder"


def rms_norm(x, g):
    xf = x.astype(jnp.float32)
    y = xf * lax.rsqrt(jnp.mean(xf * xf, axis=-1, keepdims=True) + EPS)
    return (y * g.astype(jnp.float32)).astype(x.dtype)


def swiglu(x, w_gate, w_up, w_down):
    return (jax.nn.silu(x @ w_gate) * (x @ w_up)) @ w_down


def rope_tables(positions):
    inv_freq = 1.0 / (ROPE_THETA ** (jnp.arange(0, MLA_ROPE, 2, dtype=jnp.float32) / MLA_ROPE))
    ang = positions.astype(jnp.float32)[..., None] * inv_freq
    return jnp.cos(ang), jnp.sin(ang)


def apply_rope(x, cos, sin):
    xf = x.astype(jnp.float32)
    x1, x2 = jnp.split(xf, 2, axis=-1)
    c = cos[:, :, None, :]
    s = sin[:, :, None, :]
    return jnp.concatenate([x1 * c - x2 * s, x2 * c + x1 * s], axis=-1).astype(x.dtype)


def alibi_slopes(n_heads):
    return jnp.asarray(2.0 ** (-8.0 * np.arange(1, n_heads + 1) / n_heads), dtype=jnp.float32)


def mla_mixer(x, cos, sin, w_a, q_a_g, kv_a_g, w_uq, w_ukv, q_g, k_g, w_o):
    B, S, _ = x.shape
    H = MLA_HEADS
    h = x @ w_a
    c_q = h[..., :MLA_Q_RANK]
    c_kv = h[..., MLA_Q_RANK:MLA_Q_RANK + MLA_KV_RANK]
    k_pe = h[..., MLA_Q_RANK + MLA_KV_RANK:]
    q = (rms_norm(c_q, q_a_g) @ w_uq).reshape(B, S, H, MLA_QK)
    kv = (rms_norm(c_kv, kv_a_g) @ w_ukv).reshape(B, S, H, MLA_NOPE + MLA_V)
    k_nope, v = kv[..., :MLA_NOPE], kv[..., MLA_NOPE:]
    k = jnp.concatenate([k_nope, jnp.broadcast_to(k_pe[:, :, None, :], (B, S, H, MLA_ROPE))], axis=-1)
    q = rms_norm(q, q_g)
    k = rms_norm(k, k_g)
    q = jnp.concatenate([q[..., :MLA_NOPE], apply_rope(q[..., MLA_NOPE:], cos, sin)], axis=-1)
    k = jnp.concatenate([k[..., :MLA_NOPE], apply_rope(k[..., MLA_NOPE:], cos, sin)], axis=-1)
    scale = 1.0 / math.sqrt(MLA_QK)
    nb = S // Q_BLOCK
    qb = q.reshape(B, nb, Q_BLOCK, H, MLA_QK).transpose(1, 0, 2, 3, 4)

    def block(qi):
        s = jnp.einsum('bqhd,bkhd->bhqk', qi, k).astype(jnp.float32) * scale
        p = jax.nn.softmax(s, axis=-1).astype(v.dtype)
        return jnp.einsum('bhqk,bkhd->bqhd', p, v)

    o = lax.map(block, qb)
    o = o.transpose(1, 0, 2, 3, 4).reshape(B, S, H * MLA_V)
    return o @ w_o


def band(t, nb):
    pad = [(0, 0), (BAND_BLOCK, BAND_BLOCK)] + [(0, 0)] * (t.ndim - 2)
    tp = jnp.pad(t, pad).reshape((t.shape[0], nb + 2, BAND_BLOCK) + t.shape[2:])
    return jnp.concatenate([tp[:, 0:nb], tp[:, 1:nb + 1], tp[:, 2:nb + 2]], axis=2)


def swa_mixer(x, positions, w_qkv, q_g, k_g, sink, w_o):
    B, S, _ = x.shape
    KV, G, Dh, W = SWA_KV_HEADS, SWA_GROUP, SWA_HEAD_DIM, BAND_BLOCK
    nb = S // W
    qkv = x @ w_qkv
    q = qkv[..., :SWA_Q_HEADS * Dh].reshape(B, S, KV, G, Dh)
    k = qkv[..., SWA_Q_HEADS * Dh:(SWA_Q_HEADS + KV) * Dh].reshape(B, S, KV, Dh)
    v = qkv[..., (SWA_Q_HEADS + KV) * Dh:].reshape(B, S, KV, Dh)
    q = rms_norm(q, q_g).reshape(B, nb, W, KV, G, Dh)
    k = rms_norm(k, k_g)
    kb = band(k, nb)
    vb = band(v, nb)
    pos_q = positions.reshape(B, nb, W).astype(jnp.float32)
    pos_k = band(positions, nb).astype(jnp.float32)
    valid = band(jnp.ones((1, S), dtype=bool), nb)[0]
    rel = jnp.arange(3 * W)[None, :] - W - jnp.arange(W)[:, None]
    mask = (jnp.abs(rel) <= WINDOW)[None] & valid[:, None, :]
    dist = jnp.abs(pos_q[..., :, None] - pos_k[..., None, :])
    slopes = alibi_slopes(SWA_Q_HEADS).reshape(KV, G)
    s = jnp.einsum('bnqkgd,bnskd->bnkgqs', q, kb).astype(jnp.float32) / math.sqrt(Dh)
    s = s - slopes[None, None, :, :, None, None] * dist[:, :, None, None]
    s = jnp.where(mask[None, :, None, None], s, NEG_INF)
    snk = sink.astype(jnp.float32).reshape(KV, G)[None, None, :, :, None, None]
    m = jnp.maximum(jnp.max(s, axis=-1, keepdims=True), snk)
    p = jnp.exp(s - m)
    p = p / (jnp.sum(p, axis=-1, keepdims=True) + jnp.exp(snk - m))
    o = jnp.einsum('bnkgqs,bnskd->bnqkgd', p.astype(vb.dtype), vb)
    return o.reshape(B, S, SWA_Q_HEADS * Dh) @ w_o


def setup_inputs(seed: int = 0) -> dict:
    key = jax.random.key(seed)
    ks = jax.random.split(key, 24)
    f32 = jnp.float32

    def nrm(k, shape, fan_in):
        return jax.random.normal(k, shape, f32) * (fan_in ** -0.5)

    def gain(k, shape):
        return 1.0 + 0.02 * jax.random.normal(k, shape, f32)

    x = jax.random.normal(ks[0], (BATCH, SEQ, D_MODEL), f32)
    positions = (jnp.arange(SEQ, dtype=jnp.int32)[None, :]
                 + jax.random.randint(ks[1], (BATCH, 1), 0, 1024, dtype=jnp.int32))
    L = N_MLA_LAYERS
    Ls = N_SWA_LAYERS
    return {
        "x": x,
        "positions": positions,
        "norm_g": gain(ks[2], (DEPTH, 3, D_MODEL)),
        "ffn_w_gate": nrm(ks[3], (DEPTH, 2, D_MODEL, D_FF), D_MODEL),
        "ffn_w_up": nrm(ks[4], (DEPTH, 2, D_MODEL, D_FF), D_MODEL),
        "ffn_w_down": nrm(ks[5], (DEPTH, 2, D_FF, D_MODEL), D_FF),
        "mla_w_a": nrm(ks[6], (L, D_MODEL, MLA_Q_RANK + MLA_KV_RANK + MLA_ROPE), D_MODEL),
        "mla_q_a_g": gain(ks[7], (L, MLA_Q_RANK)),
        "mla_kv_a_g": gain(ks[8], (L, MLA_KV_RANK)),
        "mla_w_uq": nrm(ks[9], (L, MLA_Q_RANK, MLA_HEADS * MLA_QK), MLA_Q_RANK),
        "mla_w_ukv": nrm(ks[10], (L, MLA_KV_RANK, MLA_HEADS * (MLA_NOPE + MLA_V)), MLA_KV_RANK),
        "mla_q_g": gain(ks[11], (L, MLA_QK)),
        "mla_k_g": gain(ks[12], (L, MLA_QK)),
        "mla_w_o": nrm(ks[13], (L, MLA_HEADS * MLA_V, D_MODEL), MLA_HEADS * MLA_V),
        "swa_w_qkv": nrm(ks[14], (Ls, D_MODEL, SWA_QKV), D_MODEL),
        "swa_q_g": gain(ks[15], (Ls, SWA_HEAD_DIM)),
        "swa_k_g": gain(ks[16], (Ls, SWA_HEAD_DIM)),
        "swa_sink": 0.5 * jax.random.normal(ks[17], (Ls, SWA_Q_HEADS), f32),
        "swa_w_o": nrm(ks[18], (Ls, SWA_Q_HEADS * SWA_HEAD_DIM, D_MODEL), SWA_Q_HEADS * SWA_HEAD_DIM),
    }


def reference(x, positions, norm_g, ffn_w_gate, ffn_w_up, ffn_w_down,
              mla_w_a, mla_q_a_g, mla_kv_a_g, mla_w_uq, mla_w_ukv, mla_q_g, mla_k_g, mla_w_o,
              swa_w_qkv, swa_q_g, swa_k_g, swa_sink, swa_w_o):
    cos, sin = rope_tables(positions)
    for i in range(DEPTH):
        g = norm_g[i]
        x = x + 0.5 * swiglu(rms_norm(x, g[0]), ffn_w_gate[i, 0], ffn_w_up[i, 0], ffn_w_down[i, 0])
        h = rms_norm(x, g[1])
        j = i // N_MIXERS
        if i % N_MIXERS == 0:
            x = x + mla_mixer(h, cos, sin, mla_w_a[j], mla_q_a_g[j], mla_kv_a_g[j],
                              mla_w_uq[j], mla_w_ukv[j], mla_q_g[j], mla_k_g[j], mla_w_o[j])
        else:
            x = x + swa_mixer(h, positions, swa_w_qkv[j], swa_q_g[j], swa_k_g[j],
                              swa_sink[j], swa_w_o[j])
        x = x + 0.5 * swiglu(rms_norm(x, g[2]), ffn_w_gate[i, 1], ffn_w_up[i, 1], ffn_w_down[i, 1])
    return x
```

```python
import functools
import math

import jax
import jax.numpy as jnp
import numpy as np
from jax import lax
from jax.experimental import pallas as pl
from jax.experimental.pallas import tpu as pltpu

F32 = jnp.float32
BF16 = jnp.bfloat16

EPS = 1e-6
NEG_BIG = -1e30
LOG2E = 1.4426950408889634
ROPE_THETA = 10000.0

MLA_HEADS = 16
MLA_NOPE = 128
MLA_ROPE = 64
MLA_QK = MLA_NOPE + MLA_ROPE
MLA_V = 128
MLA_RANK = 512
SWA_Q_HEADS = 32
SWA_KV_HEADS = 8
SWA_GROUP = SWA_Q_HEADS // SWA_KV_HEADS
SWA_DH = 64
WINDOW = 128

VMEM_LIMIT_BYTES = 58 * 1024 * 1024

FFN_TM = 1024
FFN_TF = 512
PROJ_TS = 256
ATT_TQ = 512
ATT_TK = 512
OPROJ_TS = 512
SWA_TQ = 512
ROPE_TS = 1024


def _cparams(*sem):
    return pltpu.CompilerParams(dimension_semantics=sem, vmem_limit_bytes=VMEM_LIMIT_BYTES)


def _dot(a, b):
    return jnp.dot(a, b, preferred_element_type=F32)


def _dot_nt(a, b):
    return lax.dot_general(a, b, (((1,), (1,)), ((), ())), preferred_element_type=F32)


def _dot_tn(a, b):
    return lax.dot_general(a, b, (((0,), (0,)), ((), ())), preferred_element_type=F32)


def _rms_rows(x, g):
    ms = jnp.mean(x * x, axis=-1, keepdims=True)
    return x * lax.rsqrt(ms + EPS) * g


def _rope_body(prow_ref, pcol_ref, fcol_ref, frow_ref, sgn_ref, cost_ref, sint_ref, c2_ref, s2_ref):
    ang_t = fcol_ref[...] * prow_ref[...].astype(F32)
    cost_ref[...] = jnp.cos(ang_t)
    sint_ref[...] = jnp.sin(ang_t)
    ang = pcol_ref[...].astype(F32) * frow_ref[...]
    c2_ref[...] = jnp.cos(ang)
    s2_ref[...] = jnp.sin(ang) * sgn_ref[...]


def _rope_tables(pos_row, pos_col):
    S = pos_row.shape[1]
    half = MLA_ROPE // 2
    inv_freq = 1.0 / (ROPE_THETA ** (jnp.arange(0, MLA_ROPE, 2, dtype=F32) / MLA_ROPE))
    fcol = inv_freq.reshape(half, 1)
    frow = jnp.concatenate([inv_freq, inv_freq]).reshape(1, MLA_ROPE)
    sgn = jnp.concatenate([-jnp.ones((half,), F32), jnp.ones((half,), F32)]).reshape(1, MLA_ROPE)
    ts = min(ROPE_TS, S)
    return pl.pallas_call(
        _rope_body,
        grid=(S // ts,),
        in_specs=[
            pl.BlockSpec((1, ts), lambda i: (0, i)),
            pl.BlockSpec((ts, 1), lambda i: (i, 0)),
            pl.BlockSpec((half, 1), lambda i: (0, 0)),
            pl.BlockSpec((1, MLA_ROPE), lambda i: (0, 0)),
            pl.BlockSpec((1, MLA_ROPE), lambda i: (0, 0)),
        ],
        out_specs=[
            pl.BlockSpec((half, ts), lambda i: (0, i)),
            pl.BlockSpec((half, ts), lambda i: (0, i)),
            pl.BlockSpec((ts, MLA_ROPE), lambda i: (i, 0)),
            pl.BlockSpec((ts, MLA_ROPE), lambda i: (i, 0)),
        ],
        out_shape=[
            jax.ShapeDtypeStruct((half, S), F32),
            jax.ShapeDtypeStruct((half, S), F32),
            jax.ShapeDtypeStruct((S, MLA_ROPE), F32),
            jax.ShapeDtypeStruct((S, MLA_ROPE), F32),
        ],
        compiler_params=_cparams("arbitrary"),
        name="rope_tables",
    )(pos_row, pos_col, fcol, frow, sgn)


def _ffn_body(x_ref, g_ref, wg_ref, wu_ref, wd_ref, o_ref, xn_ref):
    j = pl.program_id(1)

    @pl.when(j == 0)
    def _():
        xn_ref[...] = _rms_rows(x_ref[...], g_ref[...]).astype(BF16)
        o_ref[...] = jnp.zeros_like(o_ref)

    xn = xn_ref[...]
    gate = _dot(xn, wg_ref[...])
    up = _dot(xn, wu_ref[...])
    h = (gate * (1.0 / (1.0 + jnp.exp(-gate))) * up).astype(BF16)
    o_ref[...] += _dot(h, wd_ref[...])

    @pl.when(j == pl.num_programs(1) - 1)
    def _():
        o_ref[...] = x_ref[...] + 0.5 * o_ref[...]


def _ffn(x, g, wg, wu, wd, layer, which):
    S, D = x.shape
    F = wg.shape[3]
    tm = min(FFN_TM, S)
    tf = FFN_TF
    return pl.pallas_call(
        _ffn_body,
        grid=(S // tm, F // tf),
        in_specs=[
            pl.BlockSpec((tm, D), lambda i, j: (i, 0)),
            pl.BlockSpec((1, D), lambda i, j: (0, 0)),
            pl.BlockSpec((None, None, D, tf), lambda i, j: (layer, which, 0, j)),
            pl.BlockSpec((None, None, D, tf), lambda i, j: (layer, which, 0, j)),
            pl.BlockSpec((None, None, tf, D), lambda i, j: (layer, which, j, 0)),
        ],
        out_specs=pl.BlockSpec((tm, D), lambda i, j: (i, 0)),
        out_shape=jax.ShapeDtypeStruct((S, D), F32),
        scratch_shapes=[pltpu.VMEM((tm, D), BF16)],
        compiler_params=_cparams("arbitrary", "arbitrary"),
        name="ffn",
    )(x, g.reshape(1, D), wg, wu, wd)


def _mla_proj_body(x_ref, g_ref, wa_ref, qag_ref, kvag_ref, wuqt_ref, wkn_ref, wvt_ref,
                   qg_ref, kgn_ref, kgr_ref, kgrs_ref, cost_ref, sint_ref, c2_ref, s2_ref,
                   qt_ref, k_ref, vt_ref):
    H = MLA_HEADS
    h = _rms_rows(x_ref[...], g_ref[...]).astype(BF16)
    a = _dot(h, wa_ref[...])
    cq = _rms_rows(a[:, :MLA_RANK], qag_ref[...]).astype(BF16)
    ckv = _rms_rows(a[:, MLA_RANK:2 * MLA_RANK], kvag_ref[...]).astype(BF16)
    kpe = a[:, 2 * MLA_RANK:2 * MLA_RANK + MLA_ROPE]
    kpe_sw = a[:, 2 * MLA_RANK + MLA_ROPE:]
    ss_pe = jnp.sum(kpe * kpe, axis=1, keepdims=True)
    rk = kpe * kgr_ref[...] * c2_ref[...] + kpe_sw * kgrs_ref[...] * s2_ref[...]

    qt = _dot_nt(wuqt_ref[...], cq)
    cos_t = cost_ref[...]
    sin_t = sint_ref[...]
    qg = qg_ref[...]
    half = MLA_ROPE // 2
    for hd in range(H):
        blk = qt[hd * MLA_QK:(hd + 1) * MLA_QK]
        ms = jnp.sum(blk * blk, axis=0, keepdims=True) * (1.0 / MLA_QK)
        y = blk * lax.rsqrt(ms + EPS) * qg
        y1 = y[MLA_NOPE:MLA_NOPE + half]
        y2 = y[MLA_NOPE + half:]
        qt_ref[hd * MLA_QK:hd * MLA_QK + MLA_NOPE, :] = y[:MLA_NOPE].astype(BF16)
        qt_ref[hd * MLA_QK + MLA_NOPE:hd * MLA_QK + MLA_NOPE + half, :] = (
            y1 * cos_t - y2 * sin_t).astype(BF16)
        qt_ref[hd * MLA_QK + MLA_NOPE + half:(hd + 1) * MLA_QK, :] = (
            y2 * cos_t + y1 * sin_t).astype(BF16)

    kn = _dot(ckv, wkn_ref[...])
    kgn = kgn_ref[...]
    for hd in range(H):
        kh = kn[:, hd * MLA_NOPE:(hd + 1) * MLA_NOPE]
        ms = (jnp.sum(kh * kh, axis=1, keepdims=True) + ss_pe) * (1.0 / MLA_QK)
        r = lax.rsqrt(ms + EPS)
        k_ref[hd, :, :MLA_NOPE] = (kh * r * kgn).astype(BF16)
        k_ref[hd, :, MLA_NOPE:] = (rk * r).astype(BF16)

    vt = _dot_nt(wvt_ref[...], ckv).astype(BF16)
    for hd in range(H):
        vt_ref[hd, 0] = vt[hd * MLA_V:(hd + 1) * MLA_V]


def _mla_proj(x, g, wa_ext, qag, kvag, wuqt, wkn, wvt, qg_col, kgn, kgr, kgrs, cos_t, sin_t, c2, s2):
    S, D = x.shape
    H = MLA_HEADS
    ts = min(PROJ_TS, S)
    nc = S // ts
    half = MLA_ROPE // 2
    const = lambda i: (0, 0)
    return pl.pallas_call(
        _mla_proj_body,
        grid=(nc,),
        in_specs=[
            pl.BlockSpec((ts, D), lambda i: (i, 0)),
            pl.BlockSpec((1, D), const),
            pl.BlockSpec(wa_ext.shape, const),
            pl.BlockSpec((1, MLA_RANK), const),
            pl.BlockSpec((1, MLA_RANK), const),
            pl.BlockSpec(wuqt.shape, const),
            pl.BlockSpec(wkn.shape, const),
            pl.BlockSpec(wvt.shape, const),
            pl.BlockSpec((MLA_QK, 1), const),
            pl.BlockSpec((1, MLA_NOPE), const),
            pl.BlockSpec((1, MLA_ROPE), const),
            pl.BlockSpec((1, MLA_ROPE), const),
            pl.BlockSpec((half, ts), lambda i: (0, i)),
            pl.BlockSpec((half, ts), lambda i: (0, i)),
            pl.BlockSpec((ts, MLA_ROPE), lambda i: (i, 0)),
            pl.BlockSpec((ts, MLA_ROPE), lambda i: (i, 0)),
        ],
        out_specs=[
            pl.BlockSpec((H * MLA_QK, ts), lambda i: (0, i)),
            pl.BlockSpec((H, ts, MLA_QK), lambda i: (0, i, 0)),
            pl.BlockSpec((H, 1, MLA_V, ts), lambda i: (0, i, 0, 0)),
        ],
        out_shape=[
            jax.ShapeDtypeStruct((H * MLA_QK, S), BF16),
            jax.ShapeDtypeStruct((H, S, MLA_QK), BF16),
            jax.ShapeDtypeStruct((H, nc, MLA_V, ts), BF16),
        ],
        compiler_params=_cparams("arbitrary"),
        name="mla_proj",
    )(x, g.reshape(1, D), wa_ext, qag, kvag, wuqt, wkn, wvt, qg_col, kgn, kgr, kgrs, cos_t, sin_t, c2, s2)


def _mla_attn_body(qt_ref, k_ref, vt_ref, o_ref, *, tk, tc):
    qt = qt_ref[...]
    tq = qt.shape[1]
    n_chunks = k_ref.shape[0] // tk
    per = tk // tc

    def step(c, carry):
        m, l, acc = carry
        k = k_ref[pl.ds(pl.multiple_of(c * tk, tk), tk), :]
        s = _dot(k, qt)
        m_new = jnp.maximum(m, jnp.max(s, axis=0, keepdims=True))
        alpha = jnp.exp2(m - m_new)
        p = jnp.exp2(s - m_new)
        l = alpha * l + jnp.sum(p, axis=0, keepdims=True)
        pb = p.astype(BF16)
        pv = _dot(vt_ref[c * per], pb[:tc])
        for u in range(1, per):
            pv += _dot(vt_ref[c * per + u], pb[u * tc:(u + 1) * tc])
        return m_new, l, alpha * acc + pv

    m0 = jnp.full((1, tq), NEG_BIG, F32)
    l0 = jnp.zeros((1, tq), F32)
    acc0 = jnp.zeros((MLA_V, tq), F32)
    _, l, acc = lax.fori_loop(0, n_chunks, step, (m0, l0, acc0))
    o_ref[...] = (acc / l).T.astype(BF16)


def _mla_attn(qt, k, vt):
    H, S, _ = k.shape
    nc, tc = vt.shape[1], vt.shape[3]
    tq = min(ATT_TQ, S)
    tk = min(ATT_TK, S)
    return pl.pallas_call(
        functools.partial(_mla_attn_body, tk=tk, tc=tc),
        grid=(H, S // tq),
        in_specs=[
            pl.BlockSpec((MLA_QK, tq), lambda h, i: (h, i)),
            pl.BlockSpec((None, S, MLA_QK), lambda h, i: (h, 0, 0)),
            pl.BlockSpec((None, nc, MLA_V, tc), lambda h, i: (h, 0, 0, 0)),
        ],
        out_specs=pl.BlockSpec((tq, MLA_V), lambda h, i: (i, h)),
        out_shape=jax.ShapeDtypeStruct((S, H * MLA_V), BF16),
        compiler_params=_cparams("arbitrary", "arbitrary"),
        name="mla_attn",
    )(qt, k, vt)


def _oproj_body(x_ref, o_ref, w_ref, y_ref):
    y_ref[...] = x_ref[...] + _dot(o_ref[...], w_ref[...])


def _oproj(x, o, w):
    S, D = x.shape
    ts = min(OPROJ_TS, S)
    return pl.pallas_call(
        _oproj_body,
        grid=(S // ts,),
        in_specs=[
            pl.BlockSpec((ts, D), lambda i: (i, 0)),
            pl.BlockSpec((ts, o.shape[1]), lambda i: (i, 0)),
            pl.BlockSpec(w.shape, lambda i: (0, 0)),
        ],
        out_specs=pl.BlockSpec((ts, D), lambda i: (i, 0)),
        out_shape=jax.ShapeDtypeStruct((S, D), F32),
        compiler_params=_cparams("arbitrary"),
        name="oproj",
    )(x, o, w)


def _swa_proj_body(x_ref, g_ref, wt_ref, gcol_ref, qt_ref, kt_ref, vt_ref):
    nq = SWA_Q_HEADS * SWA_DH
    nk = SWA_KV_HEADS * SWA_DH
    nh = SWA_Q_HEADS + SWA_KV_HEADS
    h = _rms_rows(x_ref[...], g_ref[...]).astype(BF16)
    qkv = _dot_nt(wt_ref[...], h)
    ts = qkv.shape[1]
    qk = qkv[:nq + nk].reshape(nh, SWA_DH, ts)
    ms = jnp.sum(qk * qk, axis=1, keepdims=True) * (1.0 / SWA_DH)
    y = (qk * lax.rsqrt(ms + EPS)).reshape(nq + nk, ts) * gcol_ref[...]
    qt_ref[...] = y[:nq].astype(BF16)
    kt_ref[...] = y[nq:].astype(BF16)
    vt_ref[...] = qkv[nq + nk:].astype(BF16)


def _swa_proj(x, g, wt, gcol):
    S, D = x.shape
    nq = SWA_Q_HEADS * SWA_DH
    nk = SWA_KV_HEADS * SWA_DH
    ts = min(PROJ_TS, S)
    const = lambda i: (0, 0)
    return pl.pallas_call(
        _swa_proj_body,
        grid=(S // ts,),
        in_specs=[
            pl.BlockSpec((ts, D), lambda i: (i, 0)),
            pl.BlockSpec((1, D), const),
            pl.BlockSpec(wt.shape, const),
            pl.BlockSpec((nq + nk, 1), const),
        ],
        out_specs=[
            pl.BlockSpec((nq, ts), lambda i: (0, i)),
            pl.BlockSpec((nk, ts), lambda i: (0, i)),
            pl.BlockSpec((nk, ts), lambda i: (0, i)),
        ],
        out_shape=[
            jax.ShapeDtypeStruct((nq, S), BF16),
            jax.ShapeDtypeStruct((nk, S), BF16),
            jax.ShapeDtypeStruct((nk, S), BF16),
        ],
        compiler_params=_cparams("arbitrary"),
        name="swa_proj",
    )(x, g.reshape(1, D), wt, gcol)


def _alibi_slopes2():
    n = SWA_Q_HEADS
    s = (2.0 ** (-8.0 * np.arange(1, n + 1) / n)).astype(np.float32)
    return [float(v) * LOG2E for v in s]


def _swa_attn_body(qt_ref, ktp_ref, ktc_ref, ktn_ref, vtp_ref, vtc_ref, vtn_ref,
                   pqr_ref, pkp_ref, pkc_ref, pkn_ref, sink_ref, o_ref, *, seq_len):
    W = WINDOW
    G = SWA_GROUP
    dh = SWA_DH
    tq = qt_ref.shape[1]
    nsub = tq // W
    t0 = pl.program_id(0) * tq
    slopes2 = _alibi_slopes2()

    kt_all = jnp.concatenate([ktp_ref[...], ktc_ref[...], ktn_ref[...]], axis=1)
    vt_all = jnp.concatenate([vtp_ref[...], vtc_ref[...], vtn_ref[...]], axis=1)
    pk_all = jnp.concatenate([pkp_ref[...], pkc_ref[...], pkn_ref[...]], axis=0)
    pq_all = pqr_ref[...]

    for j in range(nsub):
        kidx = t0 + W * (j - 1) + lax.broadcasted_iota(jnp.int32, (3 * W, W), 0)
        qidx = t0 + W * j + lax.broadcasted_iota(jnp.int32, (3 * W, W), 1)
        rel = kidx - qidx
        ok = (rel >= -W) & (rel <= W) & (kidx >= 0) & (kidx < seq_len)
        dist = jnp.abs(pk_all[W * j:W * (j + 3)] - pq_all[:, W * j:W * (j + 1)])
        for u in range(SWA_KV_HEADS):
            ktu = kt_all[u * dh:(u + 1) * dh, W * j:W * (j + 3)]
            vtu = vt_all[u * dh:(u + 1) * dh, W * j:W * (j + 3)]
            q4 = jnp.concatenate(
                [qt_ref[(u * G + g) * dh:(u * G + g + 1) * dh, W * j:W * (j + 1)] for g in range(G)],
                axis=1)
            s = _dot_tn(ktu, q4)
            bias = jnp.concatenate(
                [jnp.where(ok, dist * (-slopes2[u * G + g]), NEG_BIG) for g in range(G)], axis=1)
            s = s + bias
            snk = sink_ref[u]
            m = jnp.maximum(jnp.max(s, axis=0, keepdims=True), snk)
            p = jnp.exp2(s - m)
            den = jnp.sum(p, axis=0, keepdims=True) + jnp.exp2(snk - m)
            o = _dot(vtu, p.astype(BF16)) / den
            for pr in range(G // 2):
                two = jnp.concatenate(
                    [o[:, (2 * pr) * W:(2 * pr + 1) * W], o[:, (2 * pr + 1) * W:(2 * pr + 2) * W]], axis=0)
                col = (u * G + 2 * pr) * dh
                o_ref[W * j:W * (j + 1), col:col + 2 * dh] = two.T.astype(BF16)


def _swa_attn(qt, kt, vt, pos_row_f, pos_col_f, sink2):
    nq, S = qt.shape
    nk = kt.shape[0]
    W = WINDOW
    tq = min(SWA_TQ, S)
    nsub = tq // W
    nblk = S // W
    cur = lambda i: (0, i)
    prev = lambda i: (0, jnp.maximum(i * nsub - 1, 0))
    nxt = lambda i: (0, jnp.minimum((i + 1) * nsub, nblk - 1))
    cur_c = lambda i: (i, 0)
    prev_c = lambda i: (jnp.maximum(i * nsub - 1, 0), 0)
    nxt_c = lambda i: (jnp.minimum((i + 1) * nsub, nblk - 1), 0)
    return pl.pallas_call(
        functools.partial(_swa_attn_body, seq_len=S),
        grid=(S // tq,),
        in_specs=[
            pl.BlockSpec((nq, tq), cur),
            pl.BlockSpec((nk, W), prev), pl.BlockSpec((nk, tq), cur), pl.BlockSpec((nk, W), nxt),
            pl.BlockSpec((nk, W), prev), pl.BlockSpec((nk, tq), cur), pl.BlockSpec((nk, W), nxt),
            pl.BlockSpec((1, tq), cur),
            pl.BlockSpec((W, 1), prev_c), pl.BlockSpec((tq, 1), cur_c), pl.BlockSpec((W, 1), nxt_c),
            pl.BlockSpec(sink2.shape, lambda i: (0, 0, 0)),
        ],
        out_specs=pl.BlockSpec((tq, nq), lambda i: (i, 0)),
        out_shape=jax.ShapeDtypeStruct((S, nq), BF16),
        compiler_params=_cparams("arbitrary"),
        name="swa_attn",
    )(qt, kt, kt, kt, vt, vt, vt, pos_row_f, pos_col_f, pos_col_f, pos_col_f, sink2)


def kernel(x, positions, norm_g, ffn_w_gate, ffn_w_up, ffn_w_down, mla_w_a, mla_q_a_g, mla_kv_a_g,
           mla_w_uq, mla_w_ukv, mla_q_g, mla_k_g, mla_w_o, swa_w_qkv, swa_q_g, swa_k_g, swa_sink, swa_w_o):
    B, S, D = x.shape
    assert B == 1
    depth = norm_g.shape[0]
    H = MLA_HEADS
    half = MLA_ROPE // 2

    pos_row = positions.reshape(1, S)
    pos_col = positions.reshape(S, 1)
    cos_t, sin_t, c2, s2 = _rope_tables(pos_row, pos_col)
    pos_row_f = pos_row.astype(F32)
    pos_col_f = pos_col.astype(F32)

    wg = ffn_w_gate.astype(BF16)
    wu = ffn_w_up.astype(BF16)
    wd = ffn_w_down.astype(BF16)

    xs = x.reshape(S, D)
    for i in range(depth):
        g = norm_g[i]
        xs = _ffn(xs, g[0], wg, wu, wd, i, 0)
        j = i // 2
        if i % 2 == 0:
            wa = mla_w_a[j]
            pe0 = 2 * MLA_RANK
            wa_ext = jnp.concatenate(
                [wa, wa[:, pe0 + half:pe0 + MLA_ROPE], wa[:, pe0:pe0 + half]], axis=1).astype(BF16)
            wuqt = mla_w_uq[j].T.astype(BF16)
            wukv = mla_w_ukv[j].reshape(MLA_RANK, H, MLA_NOPE + MLA_V)
            wkn = wukv[:, :, :MLA_NOPE].reshape(MLA_RANK, H * MLA_NOPE).astype(BF16)
            wvt = wukv[:, :, MLA_NOPE:].reshape(MLA_RANK, H * MLA_V).T.astype(BF16)
            qscale = LOG2E / math.sqrt(MLA_QK)
            qg_col = (mla_q_g[j] * qscale).reshape(MLA_QK, 1)
            kg = mla_k_g[j]
            kgn = kg[:MLA_NOPE].reshape(1, MLA_NOPE)
            kgr = kg[MLA_NOPE:].reshape(1, MLA_ROPE)
            kgrs = jnp.concatenate([kg[MLA_NOPE + half:], kg[MLA_NOPE:MLA_NOPE + half]]).reshape(1, MLA_ROPE)
            qt, k, vt = _mla_proj(xs, g[1], wa_ext, mla_q_a_g[j].reshape(1, MLA_RANK),
                                  mla_kv_a_g[j].reshape(1, MLA_RANK), wuqt, wkn, wvt,
                                  qg_col, kgn, kgr, kgrs, cos_t, sin_t, c2, s2)
            o = _mla_attn(qt, k, vt)
            xs = _oproj(xs, o, mla_w_o[j].astype(BF16))
        else:
            wt = swa_w_qkv[j].T.astype(BF16)
            qscale = LOG2E / math.sqrt(SWA_DH)
            gcol = jnp.concatenate([jnp.tile(swa_q_g[j] * qscale, SWA_Q_HEADS),
                                    jnp.tile(swa_k_g[j], SWA_KV_HEADS)]).reshape(-1, 1)
            qt, kt, vt = _swa_proj(xs, g[1], wt, gcol)
            sink2 = jnp.repeat(swa_sink[j] * LOG2E, WINDOW).reshape(SWA_KV_HEADS, 1, SWA_GROUP * WINDOW)
            o = _swa_attn(qt, kt, vt, pos_row_f, pos_col_f, sink2)
            xs = _oproj(xs, o, swa_w_o[j].astype(BF16))
        xs = _ffn(xs, g[2], wg, wu, wd, i, 1)
    return xs.reshape(B, S, D)
```

```python
import functools
import math

import jax
import jax.numpy as jnp
import numpy as np
from jax import lax
from jax.experimental import pallas as pl
from jax.experimental.pallas import tpu as pltpu

F32 = jnp.float32
BF16 = jnp.bfloat16

EPS = 1e-6
NEG_BIG = -1e30
LOG2E = 1.4426950408889634
ROPE_THETA = 10000.0

MLA_HEADS = 16
MLA_NOPE = 128
MLA_ROPE = 64
MLA_QK = MLA_NOPE + MLA_ROPE
MLA_V = 128
MLA_VE = MLA_V + 16
MLA_RANK = 512
SWA_Q_HEADS = 32
SWA_KV_HEADS = 8
SWA_GROUP = SWA_Q_HEADS // SWA_KV_HEADS
SWA_DH = 64
WINDOW = 128

VMEM_LIMIT_BYTES = 58 * 1024 * 1024

FFN_TM = 1024
FFN_TF = 512
PROJ_TS = 256
ATT_TQ = 512
ATT_TK = 512
OPROJ_TS = 512
SWA_TQ = 512
ROPE_TS = 1024


def _cparams(*sem):
    return pltpu.CompilerParams(dimension_semantics=sem, vmem_limit_bytes=VMEM_LIMIT_BYTES)


def _dot(a, b):
    return jnp.dot(a, b, preferred_element_type=F32)


def _dot_nt(a, b):
    return lax.dot_general(a, b, (((1,), (1,)), ((), ())), preferred_element_type=F32)


def _dot_tn(a, b):
    return lax.dot_general(a, b, (((0,), (0,)), ((), ())), preferred_element_type=F32)


def _rms_rows(x, g):
    ms = jnp.mean(x * x, axis=-1, keepdims=True)
    return x * lax.rsqrt(ms + EPS) * g


def _rope_body(prow_ref, pcol_ref, fcol_ref, frow_ref, sgn_ref, cost_ref, sint_ref, c2_ref, s2_ref):
    ang_t = fcol_ref[...] * prow_ref[...].astype(F32)
    cost_ref[...] = jnp.cos(ang_t)
    sint_ref[...] = jnp.sin(ang_t)
    ang = pcol_ref[...].astype(F32) * frow_ref[...]
    c2_ref[...] = jnp.cos(ang)
    s2_ref[...] = jnp.sin(ang) * sgn_ref[...]


def _rope_tables(pos_row, pos_col):
    S = pos_row.shape[1]
    half = MLA_ROPE // 2
    inv_freq = 1.0 / (ROPE_THETA ** (jnp.arange(0, MLA_ROPE, 2, dtype=F32) / MLA_ROPE))
    fcol = inv_freq.reshape(half, 1)
    frow = jnp.concatenate([inv_freq, inv_freq]).reshape(1, MLA_ROPE)
    sgn = jnp.concatenate([-jnp.ones((half,), F32), jnp.ones((half,), F32)]).reshape(1, MLA_ROPE)
    ts = min(ROPE_TS, S)
    return pl.pallas_call(
        _rope_body,
        grid=(S // ts,),
        in_specs=[
            pl.BlockSpec((1, ts), lambda i: (0, i)),
            pl.BlockSpec((ts, 1), lambda i: (i, 0)),
            pl.BlockSpec((half, 1), lambda i: (0, 0)),
            pl.BlockSpec((1, MLA_ROPE), lambda i: (0, 0)),
            pl.BlockSpec((1, MLA_ROPE), lambda i: (0, 0)),
        ],
        out_specs=[
            pl.BlockSpec((half, ts), lambda i: (0, i)),
            pl.BlockSpec((half, ts), lambda i: (0, i)),
            pl.BlockSpec((ts, MLA_ROPE), lambda i: (i, 0)),
            pl.BlockSpec((ts, MLA_ROPE), lambda i: (i, 0)),
        ],
        out_shape=[
            jax.ShapeDtypeStruct((half, S), F32),
            jax.ShapeDtypeStruct((half, S), F32),
            jax.ShapeDtypeStruct((S, MLA_ROPE), F32),
            jax.ShapeDtypeStruct((S, MLA_ROPE), F32),
        ],
        compiler_params=_cparams("arbitrary"),
        name="rope_tables",
    )(pos_row, pos_col, fcol, frow, sgn)


def _ffn_body(x_ref, g_ref, wg_ref, wu_ref, wd_ref, o_ref, xn_ref):
    j = pl.program_id(1)

    @pl.when(j == 0)
    def _():
        xn_ref[...] = _rms_rows(x_ref[...], g_ref[...]).astype(BF16)
        o_ref[...] = jnp.zeros_like(o_ref)

    xn = xn_ref[...]
    gate = _dot(xn, wg_ref[...])
    up = _dot(xn, wu_ref[...])
    h = (gate * (1.0 / (1.0 + jnp.exp(-gate))) * up).astype(BF16)
    o_ref[...] += _dot(h, wd_ref[...])

    @pl.when(j == pl.num_programs(1) - 1)
    def _():
        o_ref[...] = x_ref[...] + 0.5 * o_ref[...]


def _ffn(x, g, wg, wu, wd, layer, which):
    S, D = x.shape
    F = wg.shape[3]
    tm = min(FFN_TM, S)
    tf = FFN_TF
    return pl.pallas_call(
        _ffn_body,
        grid=(S // tm, F // tf),
        in_specs=[
            pl.BlockSpec((tm, D), lambda i, j: (i, 0)),
            pl.BlockSpec((1, D), lambda i, j: (0, 0)),
            pl.BlockSpec((None, None, D, tf), lambda i, j: (layer, which, 0, j)),
            pl.BlockSpec((None, None, D, tf), lambda i, j: (layer, which, 0, j)),
            pl.BlockSpec((None, None, tf, D), lambda i, j: (layer, which, j, 0)),
        ],
        out_specs=pl.BlockSpec((tm, D), lambda i, j: (i, 0)),
        out_shape=jax.ShapeDtypeStruct((S, D), F32),
        scratch_shapes=[pltpu.VMEM((tm, D), BF16)],
        compiler_params=_cparams("arbitrary", "arbitrary"),
        name="ffn",
    )(x, g.reshape(1, D), wg, wu, wd)


def _mla_proj_body(x_ref, g_ref, wa_ref, qag_ref, kvag_ref, wuqt_ref, wkn_ref, wvt_ref,
                   qg_ref, kgn_ref, kgr_ref, kgrs_ref, cost_ref, sint_ref, c2_ref, s2_ref,
                   qt_ref, k_ref, vt_ref):
    H = MLA_HEADS
    h = _rms_rows(x_ref[...], g_ref[...]).astype(BF16)
    a = _dot(h, wa_ref[...])
    cq = _rms_rows(a[:, :MLA_RANK], qag_ref[...]).astype(BF16)
    ckv = _rms_rows(a[:, MLA_RANK:2 * MLA_RANK], kvag_ref[...]).astype(BF16)
    kpe = a[:, 2 * MLA_RANK:2 * MLA_RANK + MLA_ROPE]
    kpe_sw = a[:, 2 * MLA_RANK + MLA_ROPE:]
    ss_pe = jnp.sum(kpe * kpe, axis=1, keepdims=True)
    rk = kpe * kgr_ref[...] * c2_ref[...] + kpe_sw * kgrs_ref[...] * s2_ref[...]

    qt = _dot_nt(wuqt_ref[...], cq)
    cos_t = cost_ref[...]
    sin_t = sint_ref[...]
    qg = qg_ref[...]
    half = MLA_ROPE // 2
    for hd in range(H):
        blk = qt[hd * MLA_QK:(hd + 1) * MLA_QK]
        ms = jnp.sum(blk * blk, axis=0, keepdims=True) * (1.0 / MLA_QK)
        y = blk * lax.rsqrt(ms + EPS) * qg
        y1 = y[MLA_NOPE:MLA_NOPE + half]
        y2 = y[MLA_NOPE + half:]
        qt_ref[hd * MLA_QK:hd * MLA_QK + MLA_NOPE, :] = y[:MLA_NOPE].astype(BF16)
        qt_ref[hd * MLA_QK + MLA_NOPE:hd * MLA_QK + MLA_NOPE + half, :] = (
            y1 * cos_t - y2 * sin_t).astype(BF16)
        qt_ref[hd * MLA_QK + MLA_NOPE + half:(hd + 1) * MLA_QK, :] = (
            y2 * cos_t + y1 * sin_t).astype(BF16)

    kn = _dot(ckv, wkn_ref[...])
    kgn = kgn_ref[...]
    for hd in range(H):
        kh = kn[:, hd * MLA_NOPE:(hd + 1) * MLA_NOPE]
        ms = (jnp.sum(kh * kh, axis=1, keepdims=True) + ss_pe) * (1.0 / MLA_QK)
        r = lax.rsqrt(ms + EPS)
        k_ref[hd, :, :MLA_NOPE] = (kh * r * kgn).astype(BF16)
        k_ref[hd, :, MLA_NOPE:] = (rk * r).astype(BF16)

    vt = _dot_nt(wvt_ref[...], ckv).astype(BF16)
    ones = jnp.ones((MLA_VE - MLA_V, vt.shape[1]), BF16)
    for hd in range(H):
        vt_ref[hd, 0, :MLA_V] = vt[hd * MLA_V:(hd + 1) * MLA_V]
        vt_ref[hd, 0, MLA_V:] = ones


def _mla_proj(x, g, wa_ext, qag, kvag, wuqt, wkn, wvt, qg_col, kgn, kgr, kgrs, cos_t, sin_t, c2, s2):
    S, D = x.shape
    H = MLA_HEADS
    ts = min(PROJ_TS, S)
    nc = S // ts
    half = MLA_ROPE // 2
    const = lambda i: (0, 0)
    return pl.pallas_call(
        _mla_proj_body,
        grid=(nc,),
        in_specs=[
            pl.BlockSpec((ts, D), lambda i: (i, 0)),
            pl.BlockSpec((1, D), const),
            pl.BlockSpec(wa_ext.shape, const),
            pl.BlockSpec((1, MLA_RANK), const),
            pl.BlockSpec((1, MLA_RANK), const),
            pl.BlockSpec(wuqt.shape, const),
            pl.BlockSpec(wkn.shape, const),
            pl.BlockSpec(wvt.shape, const),
            pl.BlockSpec((MLA_QK, 1), const),
            pl.BlockSpec((1, MLA_NOPE), const),
            pl.BlockSpec((1, MLA_ROPE), const),
            pl.BlockSpec((1, MLA_ROPE), const),
            pl.BlockSpec((half, ts), lambda i: (0, i)),
            pl.BlockSpec((half, ts), lambda i: (0, i)),
            pl.BlockSpec((ts, MLA_ROPE), lambda i: (i, 0)),
            pl.BlockSpec((ts, MLA_ROPE), lambda i: (i, 0)),
        ],
        out_specs=[
            pl.BlockSpec((H * MLA_QK, ts), lambda i: (0, i)),
            pl.BlockSpec((H, ts, MLA_QK), lambda i: (0, i, 0)),
            pl.BlockSpec((H, 1, MLA_VE, ts), lambda i: (0, i, 0, 0)),
        ],
        out_shape=[
            jax.ShapeDtypeStruct((H * MLA_QK, S), BF16),
            jax.ShapeDtypeStruct((H, S, MLA_QK), BF16),
            jax.ShapeDtypeStruct((H, nc, MLA_VE, ts), BF16),
        ],
        compiler_params=_cparams("arbitrary"),
        name="mla_proj",
    )(x, g.reshape(1, D), wa_ext, qag, kvag, wuqt, wkn, wvt, qg_col, kgn, kgr, kgrs, cos_t, sin_t, c2, s2)


def _mla_attn_body(qt_ref, k_ref, vt_ref, o_ref, *, tk, tc):
    qt = qt_ref[...]
    tq = qt.shape[1]
    n_chunks = k_ref.shape[0] // tk
    per = tk // tc

    def scores(c):
        return _dot(k_ref[c * tk:(c + 1) * tk, :], qt)

    m = jnp.full((1, tq), NEG_BIG, F32)
    acc = jnp.zeros((vt_ref.shape[1], tq), F32)
    def weighted_values(c, pb):
        pv = _dot(vt_ref[c * per], pb[:tc])
        for u in range(1, per):
            pv += _dot(vt_ref[c * per + u], pb[u * tc:(u + 1) * tc])
        return pv

    s_next = scores(0)
    pending = None
    for c in range(n_chunks):
        s = s_next
        if c + 1 < n_chunks:
            s_next = scores(c + 1)
        if pending is not None:
            acc = pending[0] * acc + weighted_values(c - 1, pending[1])
        m_new = jnp.maximum(m, jnp.max(s, axis=0, keepdims=True))
        pending = (jnp.exp2(m - m_new), jnp.exp2(s - m_new).astype(BF16))
        m = m_new
    acc = pending[0] * acc + weighted_values(n_chunks - 1, pending[1])
    o_ref[...] = (acc[:MLA_V] / acc[MLA_V:MLA_V + 1]).T.astype(BF16)


def _mla_attn(qt, k, vt):
    H, S, _ = k.shape
    nc, tc = vt.shape[1], vt.shape[3]
    tq = min(ATT_TQ, S)
    tk = min(ATT_TK, S)
    return pl.pallas_call(
        functools.partial(_mla_attn_body, tk=tk, tc=tc),
        grid=(H, S // tq),
        in_specs=[
            pl.BlockSpec((MLA_QK, tq), lambda h, i: (h, i)),
            pl.BlockSpec((None, S, MLA_QK), lambda h, i: (h, 0, 0)),
            pl.BlockSpec((None, nc, MLA_VE, tc), lambda h, i: (h, 0, 0, 0)),
        ],
        out_specs=pl.BlockSpec((tq, MLA_V), lambda h, i: (i, h)),
        out_shape=jax.ShapeDtypeStruct((S, H * MLA_V), BF16),
        compiler_params=_cparams("arbitrary", "arbitrary"),
        name="mla_attn",
    )(qt, k, vt)


def _oproj_body(x_ref, o_ref, w_ref, y_ref):
    y_ref[...] = x_ref[...] + _dot(o_ref[...], w_ref[...])


def _oproj(x, o, w):
    S, D = x.shape
    ts = min(OPROJ_TS, S)
    return pl.pallas_call(
        _oproj_body,
        grid=(S // ts,),
        in_specs=[
            pl.BlockSpec((ts, D), lambda i: (i, 0)),
            pl.BlockSpec((ts, o.shape[1]), lambda i: (i, 0)),
            pl.BlockSpec(w.shape, lambda i: (0, 0)),
        ],
        out_specs=pl.BlockSpec((ts, D), lambda i: (i, 0)),
        out_shape=jax.ShapeDtypeStruct((S, D), F32),
        compiler_params=_cparams("arbitrary"),
        name="oproj",
    )(x, o, w)


def _swa_proj_body(x_ref, g_ref, wt_ref, gcol_ref, qt_ref, kt_ref, vt_ref):
    nq = SWA_Q_HEADS * SWA_DH
    nk = SWA_KV_HEADS * SWA_DH
    nh = SWA_Q_HEADS + SWA_KV_HEADS
    h = _rms_rows(x_ref[...], g_ref[...]).astype(BF16)
    qkv = _dot_nt(wt_ref[...], h)
    ts = qkv.shape[1]
    qk = qkv[:nq + nk].reshape(nh, SWA_DH, ts)
    ms = jnp.sum(qk * qk, axis=1, keepdims=True) * (1.0 / SWA_DH)
    y = (qk * lax.rsqrt(ms + EPS)).reshape(nq + nk, ts) * gcol_ref[...]
    qt_ref[...] = y[:nq].astype(BF16)
    kt_ref[...] = y[nq:].astype(BF16)
    vt_ref[...] = qkv[nq + nk:].astype(BF16)


def _swa_proj(x, g, wt, gcol):
    S, D = x.shape
    nq = SWA_Q_HEADS * SWA_DH
    nk = SWA_KV_HEADS * SWA_DH
    ts = min(PROJ_TS, S)
    const = lambda i: (0, 0)
    return pl.pallas_call(
        _swa_proj_body,
        grid=(S // ts,),
        in_specs=[
            pl.BlockSpec((ts, D), lambda i: (i, 0)),
            pl.BlockSpec((1, D), const),
            pl.BlockSpec(wt.shape, const),
            pl.BlockSpec((nq + nk, 1), const),
        ],
        out_specs=[
            pl.BlockSpec((nq, ts), lambda i: (0, i)),
            pl.BlockSpec((nk, ts), lambda i: (0, i)),
            pl.BlockSpec((nk, ts), lambda i: (0, i)),
        ],
        out_shape=[
            jax.ShapeDtypeStruct((nq, S), BF16),
            jax.ShapeDtypeStruct((nk, S), BF16),
            jax.ShapeDtypeStruct((nk, S), BF16),
        ],
        compiler_params=_cparams("arbitrary"),
        name="swa_proj",
    )(x, g.reshape(1, D), wt, gcol)


def _alibi_slopes2():
    n = SWA_Q_HEADS
    s = (2.0 ** (-8.0 * np.arange(1, n + 1) / n)).astype(np.float32)
    return [float(v) * LOG2E for v in s]


def _swa_attn_body(qt_ref, ktp_ref, ktc_ref, ktn_ref, vtp_ref, vtc_ref, vtn_ref,
                   pqr_ref, pkp_ref, pkc_ref, pkn_ref, sink_ref, o_ref, *, seq_len):
    W = WINDOW
    G = SWA_GROUP
    dh = SWA_DH
    tq = qt_ref.shape[1]
    nsub = tq // W
    t0 = pl.program_id(0) * tq
    slopes2 = _alibi_slopes2()

    kt_all = jnp.concatenate([ktp_ref[...], ktc_ref[...], ktn_ref[...]], axis=1)
    vt_all = jnp.concatenate([vtp_ref[...], vtc_ref[...], vtn_ref[...]], axis=1)
    pk_all = jnp.concatenate([pkp_ref[...], pkc_ref[...], pkn_ref[...]], axis=0)
    pq_all = pqr_ref[...]

    for j in range(nsub):
        kidx = t0 + W * (j - 1) + lax.broadcasted_iota(jnp.int32, (3 * W, W), 0)
        qidx = t0 + W * j + lax.broadcasted_iota(jnp.int32, (3 * W, W), 1)
        rel = kidx - qidx
        ok = (rel >= -W) & (rel <= W) & (kidx >= 0) & (kidx < seq_len)
        dist = jnp.abs(pk_all[W * j:W * (j + 3)] - pq_all[:, W * j:W * (j + 1)])
        for u in range(SWA_KV_HEADS):
            ktu = kt_all[u * dh:(u + 1) * dh, W * j:W * (j + 3)]
            vtu = vt_all[u * dh:(u + 1) * dh, W * j:W * (j + 3)]
            q4 = jnp.concatenate(
                [qt_ref[(u * G + g) * dh:(u * G + g + 1) * dh, W * j:W * (j + 1)] for g in range(G)],
                axis=1)
            s = _dot_tn(ktu, q4)
            bias = jnp.concatenate(
                [jnp.where(ok, dist * (-slopes2[u * G + g]), NEG_BIG) for g in range(G)], axis=1)
            s = s + bias
            snk = sink_ref[u]
            m = jnp.maximum(jnp.max(s, axis=0, keepdims=True), snk)
            p = jnp.exp2(s - m)
            den = jnp.sum(p, axis=0, keepdims=True) + jnp.exp2(snk - m)
            o = _dot(vtu, p.astype(BF16)) / den
            for pr in range(G // 2):
                two = jnp.concatenate(
                    [o[:, (2 * pr) * W:(2 * pr + 1) * W], o[:, (2 * pr + 1) * W:(2 * pr + 2) * W]], axis=0)
                col = (u * G + 2 * pr) * dh
                o_ref[W * j:W * (j + 1), col:col + 2 * dh] = two.T.astype(BF16)


def _swa_attn(qt, kt, vt, pos_row_f, pos_col_f, sink2):
    nq, S = qt.shape
    nk = kt.shape[0]
    W = WINDOW
    tq = min(SWA_TQ, S)
    nsub = tq // W
    nblk = S // W
    cur = lambda i: (0, i)
    prev = lambda i: (0, jnp.maximum(i * nsub - 1, 0))
    nxt = lambda i: (0, jnp.minimum((i + 1) * nsub, nblk - 1))
    cur_c = lambda i: (i, 0)
    prev_c = lambda i: (jnp.maximum(i * nsub - 1, 0), 0)
    nxt_c = lambda i: (jnp.minimum((i + 1) * nsub, nblk - 1), 0)
    return pl.pallas_call(
        functools.partial(_swa_attn_body, seq_len=S),
        grid=(S // tq,),
        in_specs=[
            pl.BlockSpec((nq, tq), cur),
            pl.BlockSpec((nk, W), prev), pl.BlockSpec((nk, tq), cur), pl.BlockSpec((nk, W), nxt),
            pl.BlockSpec((nk, W), prev), pl.BlockSpec((nk, tq), cur), pl.BlockSpec((nk, W), nxt),
            pl.BlockSpec((1, tq), cur),
            pl.BlockSpec((W, 1), prev_c), pl.BlockSpec((tq, 1), cur_c), pl.BlockSpec((W, 1), nxt_c),
            pl.BlockSpec(sink2.shape, lambda i: (0, 0, 0)),
        ],
        out_specs=pl.BlockSpec((tq, nq), lambda i: (i, 0)),
        out_shape=jax.ShapeDtypeStruct((S, nq), BF16),
        compiler_params=_cparams("arbitrary"),
        name="swa_attn",
    )(qt, kt, kt, kt, vt, vt, vt, pos_row_f, pos_col_f, pos_col_f, pos_col_f, sink2)


def kernel(x, positions, norm_g, ffn_w_gate, ffn_w_up, ffn_w_down, mla_w_a, mla_q_a_g, mla_kv_a_g,
           mla_w_uq, mla_w_ukv, mla_q_g, mla_k_g, mla_w_o, swa_w_qkv, swa_q_g, swa_k_g, swa_sink, swa_w_o):
    B, S, D = x.shape
    assert B == 1
    depth = norm_g.shape[0]
    H = MLA_HEADS
    half = MLA_ROPE // 2

    pos_row = positions.reshape(1, S)
    pos_col = positions.reshape(S, 1)
    cos_t, sin_t, c2, s2 = _rope_tables(pos_row, pos_col)
    pos_row_f = pos_row.astype(F32)
    pos_col_f = pos_col.astype(F32)

    wg = ffn_w_gate.astype(BF16)
    wu = ffn_w_up.astype(BF16)
    wd = ffn_w_down.astype(BF16)

    xs = x.reshape(S, D)
    for i in range(depth):
        g = norm_g[i]
        xs = _ffn(xs, g[0], wg, wu, wd, i, 0)
        j = i // 2
        if i % 2 == 0:
            wa = mla_w_a[j]
            pe0 = 2 * MLA_RANK
            wa_ext = jnp.concatenate(
                [wa, wa[:, pe0 + half:pe0 + MLA_ROPE], wa[:, pe0:pe0 + half]], axis=1).astype(BF16)
            wuqt = mla_w_uq[j].T.astype(BF16)
            wukv = mla_w_ukv[j].reshape(MLA_RANK, H, MLA_NOPE + MLA_V)
            wkn = wukv[:, :, :MLA_NOPE].reshape(MLA_RANK, H * MLA_NOPE).astype(BF16)
            wvt = wukv[:, :, MLA_NOPE:].reshape(MLA_RANK, H * MLA_V).T.astype(BF16)
            qscale = LOG2E / math.sqrt(MLA_QK)
            qg_col = (mla_q_g[j] * qscale).reshape(MLA_QK, 1)
            kg = mla_k_g[j]
            kgn = kg[:MLA_NOPE].reshape(1, MLA_NOPE)
            kgr = kg[MLA_NOPE:].reshape(1, MLA_ROPE)
            kgrs = jnp.concatenate([kg[MLA_NOPE + half:], kg[MLA_NOPE:MLA_NOPE + half]]).reshape(1, MLA_ROPE)
            qt, k, vt = _mla_proj(xs, g[1], wa_ext, mla_q_a_g[j].reshape(1, MLA_RANK),
                                  mla_kv_a_g[j].reshape(1, MLA_RANK), wuqt, wkn, wvt,
                                  qg_col, kgn, kgr, kgrs, cos_t, sin_t, c2, s2)
            o = _mla_attn(qt, k, vt)
            xs = _oproj(xs, o, mla_w_o[j].astype(BF16))
        else:
            wt = swa_w_qkv[j].T.astype(BF16)
            qscale = LOG2E / math.sqrt(SWA_DH)
            gcol = jnp.concatenate([jnp.tile(swa_q_g[j] * qscale, SWA_Q_HEADS),
                                    jnp.tile(swa_k_g[j], SWA_KV_HEADS)]).reshape(-1, 1)
            qt, kt, vt = _swa_proj(xs, g[1], wt, gcol)
            sink2 = jnp.repeat(swa_sink[j] * LOG2E, WINDOW).reshape(SWA_KV_HEADS, 1, SWA_GROUP * WINDOW)
            o = _swa_attn(qt, kt, vt, pos_row_f, pos_col_f, sink2)
            xs = _oproj(xs, o, swa_w_o[j].astype(BF16))
        xs = _ffn(xs, g[2], wg, wu, wd, i, 1)
    return xs.reshape(B, S, D)
```

```python
import functools
import math

import jax
import jax.numpy as jnp
import numpy as np
from jax import lax
from jax.experimental import pallas as pl
from jax.experimental.pallas import tpu as pltpu

F32 = jnp.float32
BF16 = jnp.bfloat16

EPS = 1e-6
NEG_BIG = -1e30
LOG2E = 1.4426950408889634
ROPE_THETA = 10000.0
UNSHIFTED_EXP2_LIMIT = 60.0

MLA_HEADS = 16
MLA_NOPE = 128
MLA_ROPE = 64
MLA_QK = MLA_NOPE + MLA_ROPE
MLA_V = 128
MLA_RANK = 512
SWA_Q_HEADS = 32
SWA_KV_HEADS = 8
SWA_GROUP = SWA_Q_HEADS // SWA_KV_HEADS
SWA_DH = 64
WINDOW = 128

VMEM_LIMIT_BYTES = 58 * 1024 * 1024

FFN_TM = 1024
FFN_TF = 512
PROJ_TS = 256
ATT_TQ = 1024
ATT_TK = 512
OPROJ_TS = 512
SWA_TQ = 512
ROPE_TS = 1024


def _cparams(*sem):
    return pltpu.CompilerParams(dimension_semantics=sem, vmem_limit_bytes=VMEM_LIMIT_BYTES)


def _dot(a, b):
    return jnp.dot(a, b, preferred_element_type=F32)


def _dot_nt(a, b):
    return lax.dot_general(a, b, (((1,), (1,)), ((), ())), preferred_element_type=F32)


def _dot_tn(a, b):
    return lax.dot_general(a, b, (((0,), (0,)), ((), ())), preferred_element_type=F32)


def _rms_rows(x, g):
    ms = jnp.mean(x * x, axis=-1, keepdims=True)
    return x * lax.rsqrt(ms + EPS) * g


def _rope_body(prow_ref, pcol_ref, fcol_ref, frow_ref, sgn_ref, cost_ref, sint_ref, c2_ref, s2_ref):
    ang_t = fcol_ref[...] * prow_ref[...].astype(F32)
    cost_ref[...] = jnp.cos(ang_t)
    sint_ref[...] = jnp.sin(ang_t)
    ang = pcol_ref[...].astype(F32) * frow_ref[...]
    c2_ref[...] = jnp.cos(ang)
    s2_ref[...] = jnp.sin(ang) * sgn_ref[...]


def _rope_tables(pos_row, pos_col):
    S = pos_row.shape[1]
    half = MLA_ROPE // 2
    inv_freq = 1.0 / (ROPE_THETA ** (jnp.arange(0, MLA_ROPE, 2, dtype=F32) / MLA_ROPE))
    fcol = inv_freq.reshape(half, 1)
    frow = jnp.concatenate([inv_freq, inv_freq]).reshape(1, MLA_ROPE)
    sgn = jnp.concatenate([-jnp.ones((half,), F32), jnp.ones((half,), F32)]).reshape(1, MLA_ROPE)
    ts = min(ROPE_TS, S)
    return pl.pallas_call(
        _rope_body,
        grid=(S // ts,),
        in_specs=[
            pl.BlockSpec((1, ts), lambda i: (0, i)),
            pl.BlockSpec((ts, 1), lambda i: (i, 0)),
            pl.BlockSpec((half, 1), lambda i: (0, 0)),
            pl.BlockSpec((1, MLA_ROPE), lambda i: (0, 0)),
            pl.BlockSpec((1, MLA_ROPE), lambda i: (0, 0)),
        ],
        out_specs=[
            pl.BlockSpec((half, ts), lambda i: (0, i)),
            pl.BlockSpec((half, ts), lambda i: (0, i)),
            pl.BlockSpec((ts, MLA_ROPE), lambda i: (i, 0)),
            pl.BlockSpec((ts, MLA_ROPE), lambda i: (i, 0)),
        ],
        out_shape=[
            jax.ShapeDtypeStruct((half, S), F32),
            jax.ShapeDtypeStruct((half, S), F32),
            jax.ShapeDtypeStruct((S, MLA_ROPE), F32),
            jax.ShapeDtypeStruct((S, MLA_ROPE), F32),
        ],
        compiler_params=_cparams("arbitrary"),
        name="rope_tables",
    )(pos_row, pos_col, fcol, frow, sgn)


def _ffn_body(x_ref, g_ref, wg_ref, wu_ref, wd_ref, o_ref, xn_ref):
    j = pl.program_id(1)

    @pl.when(j == 0)
    def _():
        xn_ref[...] = _rms_rows(x_ref[...], g_ref[...]).astype(BF16)
        o_ref[...] = jnp.zeros_like(o_ref)

    xn = xn_ref[...]
    gate = _dot(xn, wg_ref[...])
    up = _dot(xn, wu_ref[...])
    h = (gate * (1.0 / (1.0 + jnp.exp(-gate))) * up).astype(BF16)
    o_ref[...] += _dot(h, wd_ref[...])

    @pl.when(j == pl.num_programs(1) - 1)
    def _():
        o_ref[...] = x_ref[...] + 0.5 * o_ref[...]


def _ffn(x, g, wg, wu, wd, layer, which):
    S, D = x.shape
    F = wg.shape[3]
    tm = min(FFN_TM, S)
    tf = FFN_TF
    return pl.pallas_call(
        _ffn_body,
        grid=(S // tm, F // tf),
        in_specs=[
            pl.BlockSpec((tm, D), lambda i, j: (i, 0)),
            pl.BlockSpec((1, D), lambda i, j: (0, 0)),
            pl.BlockSpec((None, None, D, tf), lambda i, j: (layer, which, 0, j)),
            pl.BlockSpec((None, None, D, tf), lambda i, j: (layer, which, 0, j)),
            pl.BlockSpec((None, None, tf, D), lambda i, j: (layer, which, j, 0)),
        ],
        out_specs=pl.BlockSpec((tm, D), lambda i, j: (i, 0)),
        out_shape=jax.ShapeDtypeStruct((S, D), F32),
        scratch_shapes=[pltpu.VMEM((tm, D), BF16)],
        compiler_params=_cparams("arbitrary", "arbitrary"),
        name="ffn",
    )(x, g.reshape(1, D), wg, wu, wd)


def _mla_proj_body(x_ref, g_ref, wa_ref, qag_ref, kvag_ref, wuqt_ref, wkn_ref, wvt_ref,
                   qg_ref, kgn_ref, kgr_ref, kgrs_ref, cost_ref, sint_ref, c2_ref, s2_ref,
                   qt_ref, k_ref, vt_ref):
    H = MLA_HEADS
    h = _rms_rows(x_ref[...], g_ref[...]).astype(BF16)
    a = _dot(h, wa_ref[...])
    cq = _rms_rows(a[:, :MLA_RANK], qag_ref[...]).astype(BF16)
    ckv = _rms_rows(a[:, MLA_RANK:2 * MLA_RANK], kvag_ref[...]).astype(BF16)
    kpe = a[:, 2 * MLA_RANK:2 * MLA_RANK + MLA_ROPE]
    kpe_sw = a[:, 2 * MLA_RANK + MLA_ROPE:]
    ss_pe = jnp.sum(kpe * kpe, axis=1, keepdims=True)
    rk = kpe * kgr_ref[...] * c2_ref[...] + kpe_sw * kgrs_ref[...] * s2_ref[...]

    qt = _dot_nt(wuqt_ref[...], cq)
    cos_t = cost_ref[...]
    sin_t = sint_ref[...]
    qg = qg_ref[...]
    half = MLA_ROPE // 2
    for hd in range(H):
        blk = qt[hd * MLA_QK:(hd + 1) * MLA_QK]
        ms = jnp.sum(blk * blk, axis=0, keepdims=True) * (1.0 / MLA_QK)
        y = blk * lax.rsqrt(ms + EPS) * qg
        y1 = y[MLA_NOPE:MLA_NOPE + half]
        y2 = y[MLA_NOPE + half:]
        qt_ref[hd * MLA_QK:hd * MLA_QK + MLA_NOPE, :] = y[:MLA_NOPE].astype(BF16)
        qt_ref[hd * MLA_QK + MLA_NOPE:hd * MLA_QK + MLA_NOPE + half, :] = (
            y1 * cos_t - y2 * sin_t).astype(BF16)
        qt_ref[hd * MLA_QK + MLA_NOPE + half:(hd + 1) * MLA_QK, :] = (
            y2 * cos_t + y1 * sin_t).astype(BF16)

    kn = _dot(ckv, wkn_ref[...])
    kgn = kgn_ref[...]
    for hd in range(H):
        kh = kn[:, hd * MLA_NOPE:(hd + 1) * MLA_NOPE]
        ms = (jnp.sum(kh * kh, axis=1, keepdims=True) + ss_pe) * (1.0 / MLA_QK)
        r = lax.rsqrt(ms + EPS)
        k_ref[hd, :, :MLA_NOPE] = (kh * r * kgn).astype(BF16)
        k_ref[hd, :, MLA_NOPE:] = (rk * r).astype(BF16)

    vt = _dot_nt(wvt_ref[...], ckv).astype(BF16)
    for hd in range(H):
        vt_ref[hd, 0] = vt[hd * MLA_V:(hd + 1) * MLA_V]


def _mla_proj(x, g, wa_ext, qag, kvag, wuqt, wkn, wvt, qg_col, kgn, kgr, kgrs, cos_t, sin_t, c2, s2):
    S, D = x.shape
    H = MLA_HEADS
    ts = min(PROJ_TS, S)
    nc = S // ts
    half = MLA_ROPE // 2
    const = lambda i: (0, 0)
    return pl.pallas_call(
        _mla_proj_body,
        grid=(nc,),
        in_specs=[
            pl.BlockSpec((ts, D), lambda i: (i, 0)),
            pl.BlockSpec((1, D), const),
            pl.BlockSpec(wa_ext.shape, const),
            pl.BlockSpec((1, MLA_RANK), const),
            pl.BlockSpec((1, MLA_RANK), const),
            pl.BlockSpec(wuqt.shape, const),
            pl.BlockSpec(wkn.shape, const),
            pl.BlockSpec(wvt.shape, const),
            pl.BlockSpec((MLA_QK, 1), const),
            pl.BlockSpec((1, MLA_NOPE), const),
            pl.BlockSpec((1, MLA_ROPE), const),
            pl.BlockSpec((1, MLA_ROPE), const),
            pl.BlockSpec((half, ts), lambda i: (0, i)),
            pl.BlockSpec((half, ts), lambda i: (0, i)),
            pl.BlockSpec((ts, MLA_ROPE), lambda i: (i, 0)),
            pl.BlockSpec((ts, MLA_ROPE), lambda i: (i, 0)),
        ],
        out_specs=[
            pl.BlockSpec((H * MLA_QK, ts), lambda i: (0, i)),
            pl.BlockSpec((H, ts, MLA_QK), lambda i: (0, i, 0)),
            pl.BlockSpec((H, 1, MLA_V, ts), lambda i: (0, i, 0, 0)),
        ],
        out_shape=[
            jax.ShapeDtypeStruct((H * MLA_QK, S), BF16),
            jax.ShapeDtypeStruct((H, S, MLA_QK), BF16),
            jax.ShapeDtypeStruct((H, nc, MLA_V, ts), BF16),
        ],
        compiler_params=_cparams("arbitrary"),
        name="mla_proj",
    )(x, g.reshape(1, D), wa_ext, qag, kvag, wuqt, wkn, wvt, qg_col, kgn, kgr, kgrs, cos_t, sin_t, c2, s2)


def _mla_attn_body(qt_ref, k_ref, vt_ref, o_ref, *, tk, tc, shift_by_max):
    qt = qt_ref[...]
    tq = qt.shape[1]
    n_chunks = k_ref.shape[0] // tk
    per = tk // tc

    def scores(c):
        return _dot(k_ref[c * tk:(c + 1) * tk, :], qt)

    def weighted_values(c, pb):
        pv = _dot(vt_ref[c * per], pb[:tc])
        for u in range(1, per):
            pv += _dot(vt_ref[c * per + u], pb[u * tc:(u + 1) * tc])
        return pv

    m = jnp.full((1, tq), NEG_BIG, F32)
    l = jnp.zeros((1, tq), F32)
    acc = jnp.zeros((MLA_V, tq), F32)
    s_next = scores(0)
    pending = None
    for c in range(n_chunks + 1):
        s = s_next
        if c + 1 < n_chunks:
            s_next = scores(c + 1)
        if pending is not None:
            alpha, pb = pending
            pv = weighted_values(c - 1, pb)
            acc = acc + pv if alpha is None else alpha * acc + pv
        if c == n_chunks:
            break
        if shift_by_max:
            m_new = jnp.maximum(m, jnp.max(s, axis=0, keepdims=True))
            alpha = jnp.exp2(m - m_new)
            p = jnp.exp2(s - m_new)
            l = alpha * l + jnp.sum(p, axis=0, keepdims=True)
            m = m_new
        else:
            alpha = None
            p = jnp.exp2(s)
            l = l + jnp.sum(p, axis=0, keepdims=True)
        pending = (alpha, p.astype(BF16))
    o_ref[...] = (acc / l).T.astype(BF16)


def _mla_attn(qt, k, vt, shift_by_max):
    H, S, _ = k.shape
    nc, tc = vt.shape[1], vt.shape[3]
    tq = min(ATT_TQ, S)
    tk = min(ATT_TK, S)
    return pl.pallas_call(
        functools.partial(_mla_attn_body, tk=tk, tc=tc, shift_by_max=shift_by_max),
        grid=(H, S // tq),
        in_specs=[
            pl.BlockSpec((MLA_QK, tq), lambda h, i: (h, i)),
            pl.BlockSpec((None, S, MLA_QK), lambda h, i: (h, 0, 0)),
            pl.BlockSpec((None, nc, MLA_V, tc), lambda h, i: (h, 0, 0, 0)),
        ],
        out_specs=pl.BlockSpec((tq, MLA_V), lambda h, i: (i, h)),
        out_shape=jax.ShapeDtypeStruct((S, H * MLA_V), BF16),
        compiler_params=_cparams("arbitrary", "arbitrary"),
        name="mla_attn",
    )(qt, k, vt)


def _oproj_body(x_ref, o_ref, w_ref, y_ref):
    y_ref[...] = x_ref[...] + _dot(o_ref[...], w_ref[...])


def _oproj(x, o, w):
    S, D = x.shape
    ts = min(OPROJ_TS, S)
    return pl.pallas_call(
        _oproj_body,
        grid=(S // ts,),
        in_specs=[
            pl.BlockSpec((ts, D), lambda i: (i, 0)),
            pl.BlockSpec((ts, o.shape[1]), lambda i: (i, 0)),
            pl.BlockSpec(w.shape, lambda i: (0, 0)),
        ],
        out_specs=pl.BlockSpec((ts, D), lambda i: (i, 0)),
        out_shape=jax.ShapeDtypeStruct((S, D), F32),
        compiler_params=_cparams("arbitrary"),
        name="oproj",
    )(x, o, w)


def _swa_proj_body(x_ref, g_ref, wt_ref, gcol_ref, qt_ref, kt_ref, vt_ref):
    nq = SWA_Q_HEADS * SWA_DH
    nk = SWA_KV_HEADS * SWA_DH
    nh = SWA_Q_HEADS + SWA_KV_HEADS
    h = _rms_rows(x_ref[...], g_ref[...]).astype(BF16)
    qkv = _dot_nt(wt_ref[...], h)
    ts = qkv.shape[1]
    qk = qkv[:nq + nk].reshape(nh, SWA_DH, ts)
    ms = jnp.sum(qk * qk, axis=1, keepdims=True) * (1.0 / SWA_DH)
    y = (qk * lax.rsqrt(ms + EPS)).reshape(nq + nk, ts) * gcol_ref[...]
    qt_ref[...] = y[:nq].astype(BF16)
    kt_ref[...] = y[nq:].astype(BF16)
    vt_ref[...] = qkv[nq + nk:].astype(BF16)


def _swa_proj(x, g, wt, gcol):
    S, D = x.shape
    nq = SWA_Q_HEADS * SWA_DH
    nk = SWA_KV_HEADS * SWA_DH
    ts = min(PROJ_TS, S)
    const = lambda i: (0, 0)
    return pl.pallas_call(
        _swa_proj_body,
        grid=(S // ts,),
        in_specs=[
            pl.BlockSpec((ts, D), lambda i: (i, 0)),
            pl.BlockSpec((1, D), const),
            pl.BlockSpec(wt.shape, const),
            pl.BlockSpec((nq + nk, 1), const),
        ],
        out_specs=[
            pl.BlockSpec((nq, ts), lambda i: (0, i)),
            pl.BlockSpec((nk, ts), lambda i: (0, i)),
            pl.BlockSpec((nk, ts), lambda i: (0, i)),
        ],
        out_shape=[
            jax.ShapeDtypeStruct((nq, S), BF16),
            jax.ShapeDtypeStruct((nk, S), BF16),
            jax.ShapeDtypeStruct((nk, S), BF16),
        ],
        compiler_params=_cparams("arbitrary"),
        name="swa_proj",
    )(x, g.reshape(1, D), wt, gcol)


def _alibi_slopes2():
    n = SWA_Q_HEADS
    s = (2.0 ** (-8.0 * np.arange(1, n + 1) / n)).astype(np.float32)
    return [float(v) * LOG2E for v in s]


def _swa_attn_body(qt_ref, ktp_ref, ktc_ref, ktn_ref, vtp_ref, vtc_ref, vtn_ref,
                   pqr_ref, pkp_ref, pkc_ref, pkn_ref, sink_ref, o_ref, *, seq_len):
    W = WINDOW
    G = SWA_GROUP
    dh = SWA_DH
    tq = qt_ref.shape[1]
    nsub = tq // W
    t0 = pl.program_id(0) * tq
    slopes2 = _alibi_slopes2()

    kt_all = jnp.concatenate([ktp_ref[...], ktc_ref[...], ktn_ref[...]], axis=1)
    vt_all = jnp.concatenate([vtp_ref[...], vtc_ref[...], vtn_ref[...]], axis=1)
    pk_all = jnp.concatenate([pkp_ref[...], pkc_ref[...], pkn_ref[...]], axis=0)
    pq_all = pqr_ref[...]

    for j in range(nsub):
        kidx = t0 + W * (j - 1) + lax.broadcasted_iota(jnp.int32, (3 * W, W), 0)
        qidx = t0 + W * j + lax.broadcasted_iota(jnp.int32, (3 * W, W), 1)
        rel = kidx - qidx
        ok = (rel >= -W) & (rel <= W) & (kidx >= 0) & (kidx < seq_len)
        dist = jnp.abs(pk_all[W * j:W * (j + 3)] - pq_all[:, W * j:W * (j + 1)])
        for u in range(SWA_KV_HEADS):
            ktu = kt_all[u * dh:(u + 1) * dh, W * j:W * (j + 3)]
            vtu = vt_all[u * dh:(u + 1) * dh, W * j:W * (j + 3)]
            q4 = jnp.concatenate(
                [qt_ref[(u * G + g) * dh:(u * G + g + 1) * dh, W * j:W * (j + 1)] for g in range(G)],
                axis=1)
            s = _dot_tn(ktu, q4)
            bias = jnp.concatenate(
                [jnp.where(ok, dist * (-slopes2[u * G + g]), NEG_BIG) for g in range(G)], axis=1)
            s = s + bias
            snk = sink_ref[u]
            m = jnp.maximum(jnp.max(s, axis=0, keepdims=True), snk)
            p = jnp.exp2(s - m)
            den = jnp.sum(p, axis=0, keepdims=True) + jnp.exp2(snk - m)
            o = _dot(vtu, p.astype(BF16)) / den
            for pr in range(G // 2):
                two = jnp.concatenate(
                    [o[:, (2 * pr) * W:(2 * pr + 1) * W], o[:, (2 * pr + 1) * W:(2 * pr + 2) * W]], axis=0)
                col = (u * G + 2 * pr) * dh
                o_ref[W * j:W * (j + 1), col:col + 2 * dh] = two.T.astype(BF16)


def _swa_attn(qt, kt, vt, pos_row_f, pos_col_f, sink2):
    nq, S = qt.shape
    nk = kt.shape[0]
    W = WINDOW
    tq = min(SWA_TQ, S)
    nsub = tq // W
    nblk = S // W
    cur = lambda i: (0, i)
    prev = lambda i: (0, jnp.maximum(i * nsub - 1, 0))
    nxt = lambda i: (0, jnp.minimum((i + 1) * nsub, nblk - 1))
    cur_c = lambda i: (i, 0)
    prev_c = lambda i: (jnp.maximum(i * nsub - 1, 0), 0)
    nxt_c = lambda i: (jnp.minimum((i + 1) * nsub, nblk - 1), 0)
    return pl.pallas_call(
        functools.partial(_swa_attn_body, seq_len=S),
        grid=(S // tq,),
        in_specs=[
            pl.BlockSpec((nq, tq), cur),
            pl.BlockSpec((nk, W), prev), pl.BlockSpec((nk, tq), cur), pl.BlockSpec((nk, W), nxt),
            pl.BlockSpec((nk, W), prev), pl.BlockSpec((nk, tq), cur), pl.BlockSpec((nk, W), nxt),
            pl.BlockSpec((1, tq), cur),
            pl.BlockSpec((W, 1), prev_c), pl.BlockSpec((tq, 1), cur_c), pl.BlockSpec((W, 1), nxt_c),
            pl.BlockSpec(sink2.shape, lambda i: (0, 0, 0)),
        ],
        out_specs=pl.BlockSpec((tq, nq), lambda i: (i, 0)),
        out_shape=jax.ShapeDtypeStruct((S, nq), BF16),
        compiler_params=_cparams("arbitrary"),
        name="swa_attn",
    )(qt, kt, kt, kt, vt, vt, vt, pos_row_f, pos_col_f, pos_col_f, pos_col_f, sink2)


def kernel(x, positions, norm_g, ffn_w_gate, ffn_w_up, ffn_w_down, mla_w_a, mla_q_a_g, mla_kv_a_g,
           mla_w_uq, mla_w_ukv, mla_q_g, mla_k_g, mla_w_o, swa_w_qkv, swa_q_g, swa_k_g, swa_sink, swa_w_o):
    B, S, D = x.shape
    assert B == 1
    depth = norm_g.shape[0]
    H = MLA_HEADS
    half = MLA_ROPE // 2

    pos_row = positions.reshape(1, S)
    pos_col = positions.reshape(S, 1)
    cos_t, sin_t, c2, s2 = _rope_tables(pos_row, pos_col)
    pos_row_f = pos_row.astype(F32)
    pos_col_f = pos_col.astype(F32)

    wg = ffn_w_gate.astype(BF16)
    wu = ffn_w_up.astype(BF16)
    wd = ffn_w_down.astype(BF16)

    xs = x.reshape(S, D)
    for i in range(depth):
        g = norm_g[i]
        xs = _ffn(xs, g[0], wg, wu, wd, i, 0)
        j = i // 2
        if i % 2 == 0:
            wa = mla_w_a[j]
            pe0 = 2 * MLA_RANK
            wa_ext = jnp.concatenate(
                [wa, wa[:, pe0 + half:pe0 + MLA_ROPE], wa[:, pe0:pe0 + half]], axis=1).astype(BF16)
            wuqt = mla_w_uq[j].T.astype(BF16)
            wukv = mla_w_ukv[j].reshape(MLA_RANK, H, MLA_NOPE + MLA_V)
            wkn = wukv[:, :, :MLA_NOPE].reshape(MLA_RANK, H * MLA_NOPE).astype(BF16)
            wvt = wukv[:, :, MLA_NOPE:].reshape(MLA_RANK, H * MLA_V).T.astype(BF16)
            qscale = LOG2E / math.sqrt(MLA_QK)
            qg_col = (mla_q_g[j] * qscale).reshape(MLA_QK, 1)
            kg = mla_k_g[j]
            kgn = kg[:MLA_NOPE].reshape(1, MLA_NOPE)
            kgr = kg[MLA_NOPE:].reshape(1, MLA_ROPE)
            kgrs = jnp.concatenate([kg[MLA_NOPE + half:], kg[MLA_NOPE:MLA_NOPE + half]]).reshape(1, MLA_ROPE)
            qt, k, vt = _mla_proj(xs, g[1], wa_ext, mla_q_a_g[j].reshape(1, MLA_RANK),
                                  mla_kv_a_g[j].reshape(1, MLA_RANK), wuqt, wkn, wvt,
                                  qg_col, kgn, kgr, kgrs, cos_t, sin_t, c2, s2)
            score_bound = (1.02 * LOG2E * math.sqrt(MLA_QK)
                           * jnp.max(jnp.abs(mla_q_g[j])) * jnp.max(jnp.abs(mla_k_g[j])))
            o = lax.cond(score_bound <= UNSHIFTED_EXP2_LIMIT,
                         lambda: _mla_attn(qt, k, vt, False),
                         lambda: _mla_attn(qt, k, vt, True))
            xs = _oproj(xs, o, mla_w_o[j].astype(BF16))
        else:
            wt = swa_w_qkv[j].T.astype(BF16)
            qscale = LOG2E / math.sqrt(SWA_DH)
            gcol = jnp.concatenate([jnp.tile(swa_q_g[j] * qscale, SWA_Q_HEADS),
                                    jnp.tile(swa_k_g[j], SWA_KV_HEADS)]).reshape(-1, 1)
            qt, kt, vt = _swa_proj(xs, g[1], wt, gcol)
            sink2 = jnp.repeat(swa_sink[j] * LOG2E, WINDOW).reshape(SWA_KV_HEADS, 1, SWA_GROUP * WINDOW)
            o = _swa_attn(qt, kt, vt, pos_row_f, pos_col_f, sink2)
            xs = _oproj(xs, o, swa_w_o[j].astype(BF16))
        xs = _ffn(xs, g[2], wg, wu, wd, i, 1)
    return xs.reshape(B, S, D)
```

```python
import functools
import math

import jax
import jax.numpy as jnp
import numpy as np
from jax import lax
from jax.experimental import pallas as pl
from jax.experimental.pallas import tpu as pltpu

F32 = jnp.float32
BF16 = jnp.bfloat16

EPS = 1e-6
NEG_BIG = -1e30
LOG2E = 1.4426950408889634
ROPE_THETA = 10000.0
UNSHIFTED_EXP2_LIMIT = 60.0

MLA_HEADS = 16
MLA_NOPE = 128
MLA_ROPE = 64
MLA_QK = MLA_NOPE + MLA_ROPE
MLA_V = 128
MLA_RANK = 512
SWA_Q_HEADS = 32
SWA_KV_HEADS = 8
SWA_GROUP = SWA_Q_HEADS // SWA_KV_HEADS
SWA_DH = 64
WINDOW = 128

VMEM_LIMIT_BYTES = 58 * 1024 * 1024

FFN_TM = 1024
FFN_TF = 256
PROJ_TS = 256
ATT_TQ = 1024
ATT_TK = 512
OPROJ_TS = 512
SWA_TQ = 512
ROPE_TS = 1024


def _cparams(*sem):
    return pltpu.CompilerParams(dimension_semantics=sem, vmem_limit_bytes=VMEM_LIMIT_BYTES)


def _dot(a, b):
    return jnp.dot(a, b, preferred_element_type=F32)


def _dot_nt(a, b):
    return lax.dot_general(a, b, (((1,), (1,)), ((), ())), preferred_element_type=F32)


def _dot_tn(a, b):
    return lax.dot_general(a, b, (((0,), (0,)), ((), ())), preferred_element_type=F32)


def _rms_rows(x, g):
    ms = jnp.mean(x * x, axis=-1, keepdims=True)
    return x * lax.rsqrt(ms + EPS) * g


def _rope_body(prow_ref, pcol_ref, fcol_ref, frow_ref, sgn_ref, cost_ref, sint_ref, c2_ref, s2_ref):
    ang_t = fcol_ref[...] * prow_ref[...].astype(F32)
    cost_ref[...] = jnp.cos(ang_t)
    sint_ref[...] = jnp.sin(ang_t)
    ang = pcol_ref[...].astype(F32) * frow_ref[...]
    c2_ref[...] = jnp.cos(ang)
    s2_ref[...] = jnp.sin(ang) * sgn_ref[...]


def _rope_tables(pos_row, pos_col):
    S = pos_row.shape[1]
    half = MLA_ROPE // 2
    inv_freq = 1.0 / (ROPE_THETA ** (jnp.arange(0, MLA_ROPE, 2, dtype=F32) / MLA_ROPE))
    fcol = inv_freq.reshape(half, 1)
    frow = jnp.concatenate([inv_freq, inv_freq]).reshape(1, MLA_ROPE)
    sgn = jnp.concatenate([-jnp.ones((half,), F32), jnp.ones((half,), F32)]).reshape(1, MLA_ROPE)
    ts = min(ROPE_TS, S)
    return pl.pallas_call(
        _rope_body,
        grid=(S // ts,),
        in_specs=[
            pl.BlockSpec((1, ts), lambda i: (0, i)),
            pl.BlockSpec((ts, 1), lambda i: (i, 0)),
            pl.BlockSpec((half, 1), lambda i: (0, 0)),
            pl.BlockSpec((1, MLA_ROPE), lambda i: (0, 0)),
            pl.BlockSpec((1, MLA_ROPE), lambda i: (0, 0)),
        ],
        out_specs=[
            pl.BlockSpec((half, ts), lambda i: (0, i)),
            pl.BlockSpec((half, ts), lambda i: (0, i)),
            pl.BlockSpec((ts, MLA_ROPE), lambda i: (i, 0)),
            pl.BlockSpec((ts, MLA_ROPE), lambda i: (i, 0)),
        ],
        out_shape=[
            jax.ShapeDtypeStruct((half, S), F32),
            jax.ShapeDtypeStruct((half, S), F32),
            jax.ShapeDtypeStruct((S, MLA_ROPE), F32),
            jax.ShapeDtypeStruct((S, MLA_ROPE), F32),
        ],
        compiler_params=_cparams("arbitrary"),
        name="rope_tables",
    )(pos_row, pos_col, fcol, frow, sgn)


def _ffn_body(x_ref, g_ref, wg_ref, wu_ref, wd_ref, o_ref, xn_ref):
    j = pl.program_id(1)

    @pl.when(j == 0)
    def _():
        x = x_ref[...]
        xn_ref[...] = _rms_rows(x, g_ref[...]).astype(BF16)
        o_ref[...] = x

    xn = xn_ref[...]
    gate = _dot(xn, wg_ref[...].astype(BF16))
    up = _dot(xn, wu_ref[...].astype(BF16))
    h = (gate * (0.5 / (1.0 + jnp.exp(-gate))) * up).astype(BF16)
    o_ref[...] += _dot(h, wd_ref[...].astype(BF16))


def _ffn(x, g, wg, wu, wd, layer, which):
    S, D = x.shape
    F = wg.shape[3]
    tm = min(FFN_TM, S)
    tf = FFN_TF
    return pl.pallas_call(
        _ffn_body,
        grid=(S // tm, F // tf),
        in_specs=[
            pl.BlockSpec((tm, D), lambda i, j: (i, 0)),
            pl.BlockSpec((1, D), lambda i, j: (0, 0)),
            pl.BlockSpec((None, None, D, tf), lambda i, j: (layer, which, 0, j)),
            pl.BlockSpec((None, None, D, tf), lambda i, j: (layer, which, 0, j)),
            pl.BlockSpec((None, None, tf, D), lambda i, j: (layer, which, j, 0)),
        ],
        out_specs=pl.BlockSpec((tm, D), lambda i, j: (i, 0)),
        out_shape=jax.ShapeDtypeStruct((S, D), F32),
        scratch_shapes=[pltpu.VMEM((tm, D), BF16)],
        compiler_params=_cparams("arbitrary", "arbitrary"),
        name="ffn",
    )(x, g.reshape(1, D), wg, wu, wd)


def _mla_proj_body(x_ref, g_ref, wa_ref, qag_ref, kvag_ref, wuqt_ref, wkn_ref, wvt_ref,
                   qg_ref, kgn_ref, kgr_ref, kgrs_ref, cost_ref, sint_ref, c2_ref, s2_ref,
                   qt_ref, k_ref, vt_ref):
    H = MLA_HEADS
    h = _rms_rows(x_ref[...], g_ref[...]).astype(BF16)
    a = _dot(h, wa_ref[...])
    cq = _rms_rows(a[:, :MLA_RANK], qag_ref[...]).astype(BF16)
    ckv = _rms_rows(a[:, MLA_RANK:2 * MLA_RANK], kvag_ref[...]).astype(BF16)
    kpe = a[:, 2 * MLA_RANK:2 * MLA_RANK + MLA_ROPE]
    kpe_sw = a[:, 2 * MLA_RANK + MLA_ROPE:]
    ss_pe = jnp.sum(kpe * kpe, axis=1, keepdims=True)
    rk = kpe * kgr_ref[...] * c2_ref[...] + kpe_sw * kgrs_ref[...] * s2_ref[...]

    qt = _dot_nt(wuqt_ref[...], cq)
    cos_t = cost_ref[...]
    sin_t = sint_ref[...]
    qg = qg_ref[...]
    half = MLA_ROPE // 2
    for hd in range(H):
        blk = qt[hd * MLA_QK:(hd + 1) * MLA_QK]
        ms = jnp.sum(blk * blk, axis=0, keepdims=True) * (1.0 / MLA_QK)
        y = blk * lax.rsqrt(ms + EPS) * qg
        y1 = y[MLA_NOPE:MLA_NOPE + half]
        y2 = y[MLA_NOPE + half:]
        qt_ref[hd * MLA_QK:hd * MLA_QK + MLA_NOPE, :] = y[:MLA_NOPE].astype(BF16)
        qt_ref[hd * MLA_QK + MLA_NOPE:hd * MLA_QK + MLA_NOPE + half, :] = (
            y1 * cos_t - y2 * sin_t).astype(BF16)
        qt_ref[hd * MLA_QK + MLA_NOPE + half:(hd + 1) * MLA_QK, :] = (
            y2 * cos_t + y1 * sin_t).astype(BF16)

    kn = _dot(ckv, wkn_ref[...])
    kgn = kgn_ref[...]
    for hd in range(H):
        kh = kn[:, hd * MLA_NOPE:(hd + 1) * MLA_NOPE]
        ms = (jnp.sum(kh * kh, axis=1, keepdims=True) + ss_pe) * (1.0 / MLA_QK)
        r = lax.rsqrt(ms + EPS)
        k_ref[hd, :, :MLA_NOPE] = (kh * r * kgn).astype(BF16)
        k_ref[hd, :, MLA_NOPE:] = (rk * r).astype(BF16)

    vt = _dot_nt(wvt_ref[...], ckv).astype(BF16)
    for hd in range(H):
        vt_ref[hd, 0] = vt[hd * MLA_V:(hd + 1) * MLA_V]


def _mla_proj(x, g, wa_ext, qag, kvag, wuqt, wkn, wvt, qg_col, kgn, kgr, kgrs, cos_t, sin_t, c2, s2):
    S, D = x.shape
    H = MLA_HEADS
    ts = min(PROJ_TS, S)
    nc = S // ts
    half = MLA_ROPE // 2
    const = lambda i: (0, 0)
    return pl.pallas_call(
        _mla_proj_body,
        grid=(nc,),
        in_specs=[
            pl.BlockSpec((ts, D), lambda i: (i, 0)),
            pl.BlockSpec((1, D), const),
            pl.BlockSpec(wa_ext.shape, const),
            pl.BlockSpec((1, MLA_RANK), const),
            pl.BlockSpec((1, MLA_RANK), const),
            pl.BlockSpec(wuqt.shape, const),
            pl.BlockSpec(wkn.shape, const),
            pl.BlockSpec(wvt.shape, const),
            pl.BlockSpec((MLA_QK, 1), const),
            pl.BlockSpec((1, MLA_NOPE), const),
            pl.BlockSpec((1, MLA_ROPE), const),
            pl.BlockSpec((1, MLA_ROPE), const),
            pl.BlockSpec((half, ts), lambda i: (0, i)),
            pl.BlockSpec((half, ts), lambda i: (0, i)),
            pl.BlockSpec((ts, MLA_ROPE), lambda i: (i, 0)),
            pl.BlockSpec((ts, MLA_ROPE), lambda i: (i, 0)),
        ],
        out_specs=[
            pl.BlockSpec((H * MLA_QK, ts), lambda i: (0, i)),
            pl.BlockSpec((H, ts, MLA_QK), lambda i: (0, i, 0)),
            pl.BlockSpec((H, 1, MLA_V, ts), lambda i: (0, i, 0, 0)),
        ],
        out_shape=[
            jax.ShapeDtypeStruct((H * MLA_QK, S), BF16),
            jax.ShapeDtypeStruct((H, S, MLA_QK), BF16),
            jax.ShapeDtypeStruct((H, nc, MLA_V, ts), BF16),
        ],
        compiler_params=_cparams("arbitrary"),
        name="mla_proj",
    )(x, g.reshape(1, D), wa_ext, qag, kvag, wuqt, wkn, wvt, qg_col, kgn, kgr, kgrs, cos_t, sin_t, c2, s2)


def _mla_attn_body(qt_ref, k_ref, vt_ref, o_ref, *, tk, tc, shift_by_max):
    qt = qt_ref[...]
    tq = qt.shape[1]
    n_chunks = k_ref.shape[0] // tk
    per = tk // tc

    def scores(c):
        return _dot(k_ref[c * tk:(c + 1) * tk, :], qt)

    def weighted_values(c, pb):
        pv = _dot(vt_ref[c * per], pb[:tc])
        for u in range(1, per):
            pv += _dot(vt_ref[c * per + u], pb[u * tc:(u + 1) * tc])
        return pv

    m = jnp.full((1, tq), NEG_BIG, F32)
    l = jnp.zeros((1, tq), F32)
    acc = jnp.zeros((MLA_V, tq), F32)
    s_next = scores(0)
    pending = None
    for c in range(n_chunks + 1):
        s = s_next
        if c + 1 < n_chunks:
            s_next = scores(c + 1)
        if pending is not None:
            alpha, pb = pending
            pv = weighted_values(c - 1, pb)
            acc = acc + pv if alpha is None else alpha * acc + pv
        if c == n_chunks:
            break
        if shift_by_max:
            m_new = jnp.maximum(m, jnp.max(s, axis=0, keepdims=True))
            alpha = jnp.exp2(m - m_new)
            p = jnp.exp2(s - m_new)
            l = alpha * l + jnp.sum(p, axis=0, keepdims=True)
            m = m_new
        else:
            alpha = None
            p = jnp.exp2(s)
            l = l + jnp.sum(p, axis=0, keepdims=True)
        pending = (alpha, p.astype(BF16))
    o_ref[...] = (acc / l).T.astype(BF16)


def _mla_attn(qt, k, vt, shift_by_max):
    H, S, _ = k.shape
    nc, tc = vt.shape[1], vt.shape[3]
    tq = min(ATT_TQ, S)
    tk = min(ATT_TK, S)
    return pl.pallas_call(
        functools.partial(_mla_attn_body, tk=tk, tc=tc, shift_by_max=shift_by_max),
        grid=(H, S // tq),
        in_specs=[
            pl.BlockSpec((MLA_QK, tq), lambda h, i: (h, i)),
            pl.BlockSpec((None, S, MLA_QK), lambda h, i: (h, 0, 0)),
            pl.BlockSpec((None, nc, MLA_V, tc), lambda h, i: (h, 0, 0, 0)),
        ],
        out_specs=pl.BlockSpec((tq, MLA_V), lambda h, i: (i, h)),
        out_shape=jax.ShapeDtypeStruct((S, H * MLA_V), BF16),
        compiler_params=_cparams("arbitrary", "arbitrary"),
        name="mla_attn",
    )(qt, k, vt)


def _oproj_body(x_ref, o_ref, w_ref, y_ref):
    y_ref[...] = x_ref[...] + _dot(o_ref[...], w_ref[...])


def _oproj(x, o, w):
    S, D = x.shape
    ts = min(OPROJ_TS, S)
    return pl.pallas_call(
        _oproj_body,
        grid=(S // ts,),
        in_specs=[
            pl.BlockSpec((ts, D), lambda i: (i, 0)),
            pl.BlockSpec((ts, o.shape[1]), lambda i: (i, 0)),
            pl.BlockSpec(w.shape, lambda i: (0, 0)),
        ],
        out_specs=pl.BlockSpec((ts, D), lambda i: (i, 0)),
        out_shape=jax.ShapeDtypeStruct((S, D), F32),
        compiler_params=_cparams("arbitrary"),
        name="oproj",
    )(x, o, w)


def _swa_proj_body(x_ref, g_ref, wt_ref, gcol_ref, qt_ref, kt_ref, vt_ref):
    nq = SWA_Q_HEADS * SWA_DH
    nk = SWA_KV_HEADS * SWA_DH
    nh = SWA_Q_HEADS + SWA_KV_HEADS
    h = _rms_rows(x_ref[...], g_ref[...]).astype(BF16)
    qkv = _dot_nt(wt_ref[...], h)
    ts = qkv.shape[1]
    qk = qkv[:nq + nk].reshape(nh, SWA_DH, ts)
    ms = jnp.sum(qk * qk, axis=1, keepdims=True) * (1.0 / SWA_DH)
    y = (qk * lax.rsqrt(ms + EPS)).reshape(nq + nk, ts) * gcol_ref[...]
    qt_ref[...] = y[:nq].astype(BF16)
    kt_ref[...] = y[nq:].astype(BF16)
    vt_ref[...] = qkv[nq + nk:].astype(BF16)


def _swa_proj(x, g, wt, gcol):
    S, D = x.shape
    nq = SWA_Q_HEADS * SWA_DH
    nk = SWA_KV_HEADS * SWA_DH
    ts = min(PROJ_TS, S)
    const = lambda i: (0, 0)
    return pl.pallas_call(
        _swa_proj_body,
        grid=(S // ts,),
        in_specs=[
            pl.BlockSpec((ts, D), lambda i: (i, 0)),
            pl.BlockSpec((1, D), const),
            pl.BlockSpec(wt.shape, const),
            pl.BlockSpec((nq + nk, 1), const),
        ],
        out_specs=[
            pl.BlockSpec((nq, ts), lambda i: (0, i)),
            pl.BlockSpec((nk, ts), lambda i: (0, i)),
            pl.BlockSpec((nk, ts), lambda i: (0, i)),
        ],
        out_shape=[
            jax.ShapeDtypeStruct((nq, S), BF16),
            jax.ShapeDtypeStruct((nk, S), BF16),
            jax.ShapeDtypeStruct((nk, S), BF16),
        ],
        compiler_params=_cparams("arbitrary"),
        name="swa_proj",
    )(x, g.reshape(1, D), wt, gcol)


def _alibi_slopes2():
    n = SWA_Q_HEADS
    s = (2.0 ** (-8.0 * np.arange(1, n + 1) / n)).astype(np.float32)
    return [float(v) * LOG2E for v in s]


def _swa_attn_body(qt_ref, ktp_ref, ktc_ref, ktn_ref, vtp_ref, vtc_ref, vtn_ref,
                   pqr_ref, pkp_ref, pkc_ref, pkn_ref, sink_ref, o_ref, *, seq_len, shift_by_max):
    W = WINDOW
    G = SWA_GROUP
    dh = SWA_DH
    tq = qt_ref.shape[1]
    nsub = tq // W
    t0 = pl.program_id(0) * tq
    slopes2 = _alibi_slopes2()

    kt_all = jnp.concatenate([ktp_ref[...], ktc_ref[...], ktn_ref[...]], axis=1)
    vt_all = jnp.concatenate([vtp_ref[...], vtc_ref[...], vtn_ref[...]], axis=1)
    pk_all = jnp.concatenate([pkp_ref[...], pkc_ref[...], pkn_ref[...]], axis=0)
    pq_all = pqr_ref[...]

    for j in range(nsub):
        kidx = t0 + W * (j - 1) + lax.broadcasted_iota(jnp.int32, (3 * W, W), 0)
        qidx = t0 + W * j + lax.broadcasted_iota(jnp.int32, (3 * W, W), 1)
        rel = kidx - qidx
        ok = (rel >= -W) & (rel <= W) & (kidx >= 0) & (kidx < seq_len)
        dist = jnp.abs(pk_all[W * j:W * (j + 3)] - pq_all[:, W * j:W * (j + 1)])
        dist = jnp.where(ok, dist, -NEG_BIG)
        for u in range(SWA_KV_HEADS):
            ktu = kt_all[u * dh:(u + 1) * dh, W * j:W * (j + 3)]
            vtu = vt_all[u * dh:(u + 1) * dh, W * j:W * (j + 3)]
            q4 = jnp.concatenate(
                [qt_ref[(u * G + g) * dh:(u * G + g + 1) * dh, W * j:W * (j + 1)] for g in range(G)],
                axis=1)
            s = _dot_tn(ktu, q4)
            s = jnp.concatenate(
                [s[:, g * W:(g + 1) * W] - dist * slopes2[u * G + g] for g in range(G)], axis=1)
            snk = sink_ref[u]
            if shift_by_max:
                m = jnp.maximum(jnp.max(s, axis=0, keepdims=True), snk)
                p = jnp.exp2(s - m)
                den = jnp.sum(p, axis=0, keepdims=True) + jnp.exp2(snk - m)
            else:
                p = jnp.exp2(s)
                den = jnp.sum(p, axis=0, keepdims=True) + jnp.exp2(snk)
            o = _dot(vtu, p.astype(BF16)) / den
            for pr in range(G // 2):
                two = jnp.concatenate(
                    [o[:, (2 * pr) * W:(2 * pr + 1) * W], o[:, (2 * pr + 1) * W:(2 * pr + 2) * W]], axis=0)
                col = (u * G + 2 * pr) * dh
                o_ref[W * j:W * (j + 1), col:col + 2 * dh] = two.T.astype(BF16)


def _swa_attn(qt, kt, vt, pos_row_f, pos_col_f, sink2, shift_by_max):
    nq, S = qt.shape
    nk = kt.shape[0]
    W = WINDOW
    tq = min(SWA_TQ, S)
    nsub = tq // W
    nblk = S // W
    cur = lambda i: (0, i)
    prev = lambda i: (0, jnp.maximum(i * nsub - 1, 0))
    nxt = lambda i: (0, jnp.minimum((i + 1) * nsub, nblk - 1))
    cur_c = lambda i: (i, 0)
    prev_c = lambda i: (jnp.maximum(i * nsub - 1, 0), 0)
    nxt_c = lambda i: (jnp.minimum((i + 1) * nsub, nblk - 1), 0)
    return pl.pallas_call(
        functools.partial(_swa_attn_body, seq_len=S, shift_by_max=shift_by_max),
        grid=(S // tq,),
        in_specs=[
            pl.BlockSpec((nq, tq), cur),
            pl.BlockSpec((nk, W), prev), pl.BlockSpec((nk, tq), cur), pl.BlockSpec((nk, W), nxt),
            pl.BlockSpec((nk, W), prev), pl.BlockSpec((nk, tq), cur), pl.BlockSpec((nk, W), nxt),
            pl.BlockSpec((1, tq), cur),
            pl.BlockSpec((W, 1), prev_c), pl.BlockSpec((tq, 1), cur_c), pl.BlockSpec((W, 1), nxt_c),
            pl.BlockSpec(sink2.shape, lambda i: (0, 0, 0)),
        ],
        out_specs=pl.BlockSpec((tq, nq), lambda i: (i, 0)),
        out_shape=jax.ShapeDtypeStruct((S, nq), BF16),
        compiler_params=_cparams("arbitrary"),
        name="swa_attn",
    )(qt, kt, kt, kt, vt, vt, vt, pos_row_f, pos_col_f, pos_col_f, pos_col_f, sink2)


def kernel(x, positions, norm_g, ffn_w_gate, ffn_w_up, ffn_w_down, mla_w_a, mla_q_a_g, mla_kv_a_g,
           mla_w_uq, mla_w_ukv, mla_q_g, mla_k_g, mla_w_o, swa_w_qkv, swa_q_g, swa_k_g, swa_sink, swa_w_o):
    B, S, D = x.shape
    assert B == 1
    depth = norm_g.shape[0]
    H = MLA_HEADS
    half = MLA_ROPE // 2

    pos_row = positions.reshape(1, S)
    pos_col = positions.reshape(S, 1)
    cos_t, sin_t, c2, s2 = _rope_tables(pos_row, pos_col)
    pos_row_f = pos_row.astype(F32)
    pos_col_f = pos_col.astype(F32)

    wg, wu, wd = ffn_w_gate, ffn_w_up, ffn_w_down

    xs = x.reshape(S, D)
    for i in range(depth):
        g = norm_g[i]
        xs = _ffn(xs, g[0], wg, wu, wd, i, 0)
        j = i // 2
        if i % 2 == 0:
            wa = mla_w_a[j]
            pe0 = 2 * MLA_RANK
            wa_ext = jnp.concatenate(
                [wa, wa[:, pe0 + half:pe0 + MLA_ROPE], wa[:, pe0:pe0 + half]], axis=1).astype(BF16)
            wuqt = mla_w_uq[j].T.astype(BF16)
            wukv = mla_w_ukv[j].reshape(MLA_RANK, H, MLA_NOPE + MLA_V)
            wkn = wukv[:, :, :MLA_NOPE].reshape(MLA_RANK, H * MLA_NOPE).astype(BF16)
            wvt = wukv[:, :, MLA_NOPE:].reshape(MLA_RANK, H * MLA_V).T.astype(BF16)
            qscale = LOG2E / math.sqrt(MLA_QK)
            qg_col = (mla_q_g[j] * qscale).reshape(MLA_QK, 1)
            kg = mla_k_g[j]
            kgn = kg[:MLA_NOPE].reshape(1, MLA_NOPE)
            kgr = kg[MLA_NOPE:].reshape(1, MLA_ROPE)
            kgrs = jnp.concatenate([kg[MLA_NOPE + half:], kg[MLA_NOPE:MLA_NOPE + half]]).reshape(1, MLA_ROPE)
            qt, k, vt = _mla_proj(xs, g[1], wa_ext, mla_q_a_g[j].reshape(1, MLA_RANK),
                                  mla_kv_a_g[j].reshape(1, MLA_RANK), wuqt, wkn, wvt,
                                  qg_col, kgn, kgr, kgrs, cos_t, sin_t, c2, s2)
            score_bound = (1.02 * LOG2E * math.sqrt(MLA_QK)
                           * jnp.max(jnp.abs(mla_q_g[j])) * jnp.max(jnp.abs(mla_k_g[j])))
            o = lax.cond(score_bound <= UNSHIFTED_EXP2_LIMIT,
                         lambda: _mla_attn(qt, k, vt, False),
                         lambda: _mla_attn(qt, k, vt, True))
            xs = _oproj(xs, o, mla_w_o[j].astype(BF16))
        else:
            wt = swa_w_qkv[j].T.astype(BF16)
            qscale = LOG2E / math.sqrt(SWA_DH)
            gcol = jnp.concatenate([jnp.tile(swa_q_g[j] * qscale, SWA_Q_HEADS),
                                    jnp.tile(swa_k_g[j], SWA_KV_HEADS)]).reshape(-1, 1)
            qt, kt, vt = _swa_proj(xs, g[1], wt, gcol)
            sink2 = jnp.repeat(swa_sink[j] * LOG2E, WINDOW).reshape(SWA_KV_HEADS, 1, SWA_GROUP * WINDOW)
            score_bound = jnp.maximum(
                1.02 * LOG2E * math.sqrt(SWA_DH) * jnp.max(jnp.abs(swa_q_g[j])) * jnp.max(jnp.abs(swa_k_g[j])),
                jnp.max(jnp.abs(sink2)))
            o = lax.cond(score_bound <= UNSHIFTED_EXP2_LIMIT,
                         lambda: _swa_attn(qt, kt, vt, pos_row_f, pos_col_f, sink2, False),
                         lambda: _swa_attn(qt, kt, vt, pos_row_f, pos_col_f, sink2, True))
            xs = _oproj(xs, o, swa_w_o[j].astype(BF16))
        xs = _ffn(xs, g[2], wg, wu, wd, i, 1)
    return xs.reshape(B, S, D)
```

```python
import functools
import math

import jax
import jax.numpy as jnp
import numpy as np
from jax import lax
from jax.experimental import pallas as pl
from jax.experimental.pallas import tpu as pltpu

F32 = jnp.float32
BF16 = jnp.bfloat16

EPS = 1e-6
NEG_BIG = -1e30
LOG2E = 1.4426950408889634
ROPE_THETA = 10000.0
UNSHIFTED_EXP2_LIMIT = 60.0

MLA_HEADS = 16
MLA_NOPE = 128
MLA_ROPE = 64
MLA_QK = MLA_NOPE + MLA_ROPE
MLA_V = 128
MLA_RANK = 512
SWA_Q_HEADS = 32
SWA_KV_HEADS = 8
SWA_GROUP = SWA_Q_HEADS // SWA_KV_HEADS
SWA_DH = 64
WINDOW = 128

VMEM_LIMIT_BYTES = 58 * 1024 * 1024

FFN_TM = 1024
FFN_TF = 256
FFN_SLOTS = 3
PROJ_TS = 256
ATT_TQ = 1024
ATT_TK = 512
OPROJ_TS = 512
SWA_TQ = 512
ROPE_TS = 1024


def _cparams(*sem):
    return pltpu.CompilerParams(dimension_semantics=sem, vmem_limit_bytes=VMEM_LIMIT_BYTES)


def _dot(a, b):
    return jnp.dot(a, b, preferred_element_type=F32)


def _dot_nt(a, b):
    return lax.dot_general(a, b, (((1,), (1,)), ((), ())), preferred_element_type=F32)


def _dot_tn(a, b):
    return lax.dot_general(a, b, (((0,), (0,)), ((), ())), preferred_element_type=F32)


def _rms_rows(x, g):
    ms = jnp.mean(x * x, axis=-1, keepdims=True)
    return x * lax.rsqrt(ms + EPS) * g


def _rope_body(prow_ref, pcol_ref, fcol_ref, frow_ref, sgn_ref, cost_ref, sint_ref, c2_ref, s2_ref):
    ang_t = fcol_ref[...] * prow_ref[...].astype(F32)
    cost_ref[...] = jnp.cos(ang_t)
    sint_ref[...] = jnp.sin(ang_t)
    ang = pcol_ref[...].astype(F32) * frow_ref[...]
    c2_ref[...] = jnp.cos(ang)
    s2_ref[...] = jnp.sin(ang) * sgn_ref[...]


def _rope_tables(pos_row, pos_col):
    S = pos_row.shape[1]
    half = MLA_ROPE // 2
    inv_freq = 1.0 / (ROPE_THETA ** (jnp.arange(0, MLA_ROPE, 2, dtype=F32) / MLA_ROPE))
    fcol = inv_freq.reshape(half, 1)
    frow = jnp.concatenate([inv_freq, inv_freq]).reshape(1, MLA_ROPE)
    sgn = jnp.concatenate([-jnp.ones((half,), F32), jnp.ones((half,), F32)]).reshape(1, MLA_ROPE)
    ts = min(ROPE_TS, S)
    return pl.pallas_call(
        _rope_body,
        grid=(S // ts,),
        in_specs=[
            pl.BlockSpec((1, ts), lambda i: (0, i)),
            pl.BlockSpec((ts, 1), lambda i: (i, 0)),
            pl.BlockSpec((half, 1), lambda i: (0, 0)),
            pl.BlockSpec((1, MLA_ROPE), lambda i: (0, 0)),
            pl.BlockSpec((1, MLA_ROPE), lambda i: (0, 0)),
        ],
        out_specs=[
            pl.BlockSpec((half, ts), lambda i: (0, i)),
            pl.BlockSpec((half, ts), lambda i: (0, i)),
            pl.BlockSpec((ts, MLA_ROPE), lambda i: (i, 0)),
            pl.BlockSpec((ts, MLA_ROPE), lambda i: (i, 0)),
        ],
        out_shape=[
            jax.ShapeDtypeStruct((half, S), F32),
            jax.ShapeDtypeStruct((half, S), F32),
            jax.ShapeDtypeStruct((S, MLA_ROPE), F32),
            jax.ShapeDtypeStruct((S, MLA_ROPE), F32),
        ],
        compiler_params=_cparams("arbitrary"),
        name="rope_tables",
    )(pos_row, pos_col, fcol, frow, sgn)


def _ffn_body(g_ref, x_hbm, wg_hbm, wu_hbm, wd_hbm, o_ref, xbuf, xn_ref, gbuf, ubuf, dbuf, wsem, xsem,
              *, layer, which, tf, n_chunks):
    i = pl.program_id(0)
    n_tiles = pl.num_programs(0)
    tm = xbuf.shape[0]
    total = n_tiles * n_chunks
    ahead = FFN_SLOTS - 1

    def weight_copies(q):
        slot = lax.rem(q, FFN_SLOTS)
        cols = pl.ds(pl.multiple_of(lax.rem(q, n_chunks) * tf, tf), tf)
        return (pltpu.make_async_copy(wg_hbm.at[layer, which, :, cols], gbuf.at[slot], wsem.at[0, slot]),
                pltpu.make_async_copy(wu_hbm.at[layer, which, :, cols], ubuf.at[slot], wsem.at[1, slot]),
                pltpu.make_async_copy(wd_hbm.at[layer, which, cols, :], dbuf.at[slot], wsem.at[2, slot]))

    def x_copy(tile):
        rows = pl.ds(pl.multiple_of(tile * tm, tm), tm)
        return pltpu.make_async_copy(x_hbm.at[rows, :], xbuf, xsem.at[0])

    @pl.when(i == 0)
    def _():
        x_copy(0).start()
        for q in range(ahead):
            for cp in weight_copies(q):
                cp.start()

    x_copy(i).wait()
    x = xbuf[...]
    xn_ref[...] = _rms_rows(x, g_ref[...]).astype(BF16)
    o_ref[...] = x

    @pl.when(i + 1 < n_tiles)
    def _():
        x_copy(i + 1).start()

    def chunk(c, carry):
        q = i * n_chunks + c
        for cp in weight_copies(q):
            cp.wait()

        @pl.when(q + ahead < total)
        def _():
            for cp in weight_copies(q + ahead):
                cp.start()

        slot = lax.rem(q, FFN_SLOTS)
        xn = xn_ref[...]
        gate = _dot(xn, gbuf[slot].astype(BF16))
        up = _dot(xn, ubuf[slot].astype(BF16))
        h = (gate * (0.5 / (1.0 + jnp.exp(-gate))) * up).astype(BF16)
        o_ref[...] += _dot(h, dbuf[slot].astype(BF16))
        return carry

    lax.fori_loop(0, n_chunks, chunk, 0, unroll=2)


def _ffn(x, g, wg, wu, wd, layer, which):
    S, D = x.shape
    F = wg.shape[3]
    tm = min(FFN_TM, S)
    tf = FFN_TF
    n_chunks = F // tf
    assert n_chunks * tf == F and n_chunks >= FFN_SLOTS and S % tm == 0
    return pl.pallas_call(
        functools.partial(_ffn_body, layer=layer, which=which, tf=tf, n_chunks=n_chunks),
        grid=(S // tm,),
        in_specs=[
            pl.BlockSpec((1, D), lambda i: (0, 0)),
            pl.BlockSpec(memory_space=pl.ANY),
            pl.BlockSpec(memory_space=pl.ANY),
            pl.BlockSpec(memory_space=pl.ANY),
            pl.BlockSpec(memory_space=pl.ANY),
        ],
        out_specs=pl.BlockSpec((tm, D), lambda i: (i, 0)),
        out_shape=jax.ShapeDtypeStruct((S, D), F32),
        scratch_shapes=[
            pltpu.VMEM((tm, D), F32),
            pltpu.VMEM((tm, D), BF16),
            pltpu.VMEM((FFN_SLOTS, D, tf), F32),
            pltpu.VMEM((FFN_SLOTS, D, tf), F32),
            pltpu.VMEM((FFN_SLOTS, tf, D), F32),
            pltpu.SemaphoreType.DMA((3, FFN_SLOTS)),
            pltpu.SemaphoreType.DMA((1,)),
        ],
        compiler_params=_cparams("arbitrary"),
        name="ffn",
    )(g.reshape(1, D), x, wg, wu, wd)


def _mla_proj_body(x_ref, g_ref, wa_ref, qag_ref, kvag_ref, wuqt_ref, wkn_ref, wvt_ref,
                   qg_ref, kgn_ref, kgr_ref, kgrs_ref, cost_ref, sint_ref, c2_ref, s2_ref,
                   qt_ref, k_ref, vt_ref):
    H = MLA_HEADS
    h = _rms_rows(x_ref[...], g_ref[...]).astype(BF16)
    a = _dot(h, wa_ref[...])
    cq = _rms_rows(a[:, :MLA_RANK], qag_ref[...]).astype(BF16)
    ckv = _rms_rows(a[:, MLA_RANK:2 * MLA_RANK], kvag_ref[...]).astype(BF16)
    kpe = a[:, 2 * MLA_RANK:2 * MLA_RANK + MLA_ROPE]
    kpe_sw = a[:, 2 * MLA_RANK + MLA_ROPE:]
    ss_pe = jnp.sum(kpe * kpe, axis=1, keepdims=True)
    rk = kpe * kgr_ref[...] * c2_ref[...] + kpe_sw * kgrs_ref[...] * s2_ref[...]

    qt = _dot_nt(wuqt_ref[...], cq)
    cos_t = cost_ref[...]
    sin_t = sint_ref[...]
    qg = qg_ref[...]
    half = MLA_ROPE // 2
    for hd in range(H):
        blk = qt[hd * MLA_QK:(hd + 1) * MLA_QK]
        ms = jnp.sum(blk * blk, axis=0, keepdims=True) * (1.0 / MLA_QK)
        y = blk * lax.rsqrt(ms + EPS) * qg
        y1 = y[MLA_NOPE:MLA_NOPE + half]
        y2 = y[MLA_NOPE + half:]
        qt_ref[hd * MLA_QK:hd * MLA_QK + MLA_NOPE, :] = y[:MLA_NOPE].astype(BF16)
        qt_ref[hd * MLA_QK + MLA_NOPE:hd * MLA_QK + MLA_NOPE + half, :] = (
            y1 * cos_t - y2 * sin_t).astype(BF16)
        qt_ref[hd * MLA_QK + MLA_NOPE + half:(hd + 1) * MLA_QK, :] = (
            y2 * cos_t + y1 * sin_t).astype(BF16)

    kn = _dot(ckv, wkn_ref[...])
    kgn = kgn_ref[...]
    for hd in range(H):
        kh = kn[:, hd * MLA_NOPE:(hd + 1) * MLA_NOPE]
        ms = (jnp.sum(kh * kh, axis=1, keepdims=True) + ss_pe) * (1.0 / MLA_QK)
        r = lax.rsqrt(ms + EPS)
        k_ref[hd, :, :MLA_NOPE] = (kh * r * kgn).astype(BF16)
        k_ref[hd, :, MLA_NOPE:] = (rk * r).astype(BF16)

    vt = _dot_nt(wvt_ref[...], ckv).astype(BF16)
    for hd in range(H):
        vt_ref[hd, 0] = vt[hd * MLA_V:(hd + 1) * MLA_V]


def _mla_proj(x, g, wa_ext, qag, kvag, wuqt, wkn, wvt, qg_col, kgn, kgr, kgrs, cos_t, sin_t, c2, s2):
    S, D = x.shape
    H = MLA_HEADS
    ts = min(PROJ_TS, S)
    nc = S // ts
    half = MLA_ROPE // 2
    const = lambda i: (0, 0)
    return pl.pallas_call(
        _mla_proj_body,
        grid=(nc,),
        in_specs=[
            pl.BlockSpec((ts, D), lambda i: (i, 0)),
            pl.BlockSpec((1, D), const),
            pl.BlockSpec(wa_ext.shape, const),
            pl.BlockSpec((1, MLA_RANK), const),
            pl.BlockSpec((1, MLA_RANK), const),
            pl.BlockSpec(wuqt.shape, const),
            pl.BlockSpec(wkn.shape, const),
            pl.BlockSpec(wvt.shape, const),
            pl.BlockSpec((MLA_QK, 1), const),
            pl.BlockSpec((1, MLA_NOPE), const),
            pl.BlockSpec((1, MLA_ROPE), const),
            pl.BlockSpec((1, MLA_ROPE), const),
            pl.BlockSpec((half, ts), lambda i: (0, i)),
            pl.BlockSpec((half, ts), lambda i: (0, i)),
            pl.BlockSpec((ts, MLA_ROPE), lambda i: (i, 0)),
            pl.BlockSpec((ts, MLA_ROPE), lambda i: (i, 0)),
        ],
        out_specs=[
            pl.BlockSpec((H * MLA_QK, ts), lambda i: (0, i)),
            pl.BlockSpec((H, ts, MLA_QK), lambda i: (0, i, 0)),
            pl.BlockSpec((H, 1, MLA_V, ts), lambda i: (0, i, 0, 0)),
        ],
        out_shape=[
            jax.ShapeDtypeStruct((H * MLA_QK, S), BF16),
            jax.ShapeDtypeStruct((H, S, MLA_QK), BF16),
            jax.ShapeDtypeStruct((H, nc, MLA_V, ts), BF16),
        ],
        compiler_params=_cparams("arbitrary"),
        name="mla_proj",
    )(x, g.reshape(1, D), wa_ext, qag, kvag, wuqt, wkn, wvt, qg_col, kgn, kgr, kgrs, cos_t, sin_t, c2, s2)


def _mla_attn_body(qt_ref, k_ref, vt_ref, o_ref, *, tk, tc, shift_by_max):
    qt = qt_ref[...]
    tq = qt.shape[1]
    n_chunks = k_ref.shape[0] // tk
    per = tk // tc

    def scores(c):
        return _dot(k_ref[c * tk:(c + 1) * tk, :], qt)

    def weighted_values(c, pb):
        pv = _dot(vt_ref[c * per], pb[:tc])
        for u in range(1, per):
            pv += _dot(vt_ref[c * per + u], pb[u * tc:(u + 1) * tc])
        return pv

    m = jnp.full((1, tq), NEG_BIG, F32)
    l = jnp.zeros((1, tq), F32)
    acc = jnp.zeros((MLA_V, tq), F32)
    s_next = scores(0)
    pending = None
    for c in range(n_chunks + 1):
        s = s_next
        if c + 1 < n_chunks:
            s_next = scores(c + 1)
        if pending is not None:
            alpha, pb = pending
            pv = weighted_values(c - 1, pb)
            acc = acc + pv if alpha is None else alpha * acc + pv
        if c == n_chunks:
            break
        if shift_by_max:
            m_new = jnp.maximum(m, jnp.max(s, axis=0, keepdims=True))
            alpha = jnp.exp2(m - m_new)
            p = jnp.exp2(s - m_new)
            l = alpha * l + jnp.sum(p, axis=0, keepdims=True)
            m = m_new
        else:
            alpha = None
            p = jnp.exp2(s)
            l = l + jnp.sum(p, axis=0, keepdims=True)
        pending = (alpha, p.astype(BF16))
    o_ref[...] = (acc / l).T.astype(BF16)


def _mla_attn(qt, k, vt, shift_by_max):
    H, S, _ = k.shape
    nc, tc = vt.shape[1], vt.shape[3]
    tq = min(ATT_TQ, S)
    tk = min(ATT_TK, S)
    return pl.pallas_call(
        functools.partial(_mla_attn_body, tk=tk, tc=tc, shift_by_max=shift_by_max),
        grid=(H, S // tq),
        in_specs=[
            pl.BlockSpec((MLA_QK, tq), lambda h, i: (h, i)),
            pl.BlockSpec((None, S, MLA_QK), lambda h, i: (h, 0, 0)),
            pl.BlockSpec((None, nc, MLA_V, tc), lambda h, i: (h, 0, 0, 0)),
        ],
        out_specs=pl.BlockSpec((tq, MLA_V), lambda h, i: (i, h)),
        out_shape=jax.ShapeDtypeStruct((S, H * MLA_V), BF16),
        compiler_params=_cparams("arbitrary", "arbitrary"),
        name="mla_attn",
    )(qt, k, vt)


def _oproj_body(x_ref, o_ref, w_ref, y_ref):
    y_ref[...] = x_ref[...] + _dot(o_ref[...], w_ref[...])


def _oproj(x, o, w):
    S, D = x.shape
    ts = min(OPROJ_TS, S)
    return pl.pallas_call(
        _oproj_body,
        grid=(S // ts,),
        in_specs=[
            pl.BlockSpec((ts, D), lambda i: (i, 0)),
            pl.BlockSpec((ts, o.shape[1]), lambda i: (i, 0)),
            pl.BlockSpec(w.shape, lambda i: (0, 0)),
        ],
        out_specs=pl.BlockSpec((ts, D), lambda i: (i, 0)),
        out_shape=jax.ShapeDtypeStruct((S, D), F32),
        compiler_params=_cparams("arbitrary"),
        name="oproj",
    )(x, o, w)


def _swa_proj_body(x_ref, g_ref, wt_ref, gcol_ref, qt_ref, kt_ref, vt_ref):
    nq = SWA_Q_HEADS * SWA_DH
    nk = SWA_KV_HEADS * SWA_DH
    nh = SWA_Q_HEADS + SWA_KV_HEADS
    h = _rms_rows(x_ref[...], g_ref[...]).astype(BF16)
    qkv = _dot_nt(wt_ref[...], h)
    ts = qkv.shape[1]
    qk = qkv[:nq + nk].reshape(nh, SWA_DH, ts)
    ms = jnp.sum(qk * qk, axis=1, keepdims=True) * (1.0 / SWA_DH)
    y = (qk * lax.rsqrt(ms + EPS)).reshape(nq + nk, ts) * gcol_ref[...]
    qt_ref[...] = y[:nq].astype(BF16)
    kt_ref[...] = y[nq:].astype(BF16)
    vt_ref[...] = qkv[nq + nk:].astype(BF16)


def _swa_proj(x, g, wt, gcol):
    S, D = x.shape
    nq = SWA_Q_HEADS * SWA_DH
    nk = SWA_KV_HEADS * SWA_DH
    ts = min(PROJ_TS, S)
    const = lambda i: (0, 0)
    return pl.pallas_call(
        _swa_proj_body,
        grid=(S // ts,),
        in_specs=[
            pl.BlockSpec((ts, D), lambda i: (i, 0)),
            pl.BlockSpec((1, D), const),
            pl.BlockSpec(wt.shape, const),
            pl.BlockSpec((nq + nk, 1), const),
        ],
        out_specs=[
            pl.BlockSpec((nq, ts), lambda i: (0, i)),
            pl.BlockSpec((nk, ts), lambda i: (0, i)),
            pl.BlockSpec((nk, ts), lambda i: (0, i)),
        ],
        out_shape=[
            jax.ShapeDtypeStruct((nq, S), BF16),
            jax.ShapeDtypeStruct((nk, S), BF16),
            jax.ShapeDtypeStruct((nk, S), BF16),
        ],
        compiler_params=_cparams("arbitrary"),
        name="swa_proj",
    )(x, g.reshape(1, D), wt, gcol)


def _alibi_slopes2():
    n = SWA_Q_HEADS
    s = (2.0 ** (-8.0 * np.arange(1, n + 1) / n)).astype(np.float32)
    return [float(v) * LOG2E for v in s]


def _swa_attn_body(qt_ref, ktp_ref, ktc_ref, ktn_ref, vtp_ref, vtc_ref, vtn_ref,
                   pqr_ref, pkp_ref, pkc_ref, pkn_ref, sink_ref, o_ref, *, seq_len, shift_by_max):
    W = WINDOW
    G = SWA_GROUP
    dh = SWA_DH
    tq = qt_ref.shape[1]
    nsub = tq // W
    t0 = pl.program_id(0) * tq
    slopes2 = _alibi_slopes2()

    kt_all = jnp.concatenate([ktp_ref[...], ktc_ref[...], ktn_ref[...]], axis=1)
    vt_all = jnp.concatenate([vtp_ref[...], vtc_ref[...], vtn_ref[...]], axis=1)
    pk_all = jnp.concatenate([pkp_ref[...], pkc_ref[...], pkn_ref[...]], axis=0)
    pq_all = pqr_ref[...]

    for j in range(nsub):
        kidx = t0 + W * (j - 1) + lax.broadcasted_iota(jnp.int32, (3 * W, W), 0)
        qidx = t0 + W * j + lax.broadcasted_iota(jnp.int32, (3 * W, W), 1)
        rel = kidx - qidx
        ok = (rel >= -W) & (rel <= W) & (kidx >= 0) & (kidx < seq_len)
        dist = jnp.abs(pk_all[W * j:W * (j + 3)] - pq_all[:, W * j:W * (j + 1)])
        dist = jnp.where(ok, dist, -NEG_BIG)
        for u in range(SWA_KV_HEADS):
            ktu = kt_all[u * dh:(u + 1) * dh, W * j:W * (j + 3)]
            vtu = vt_all[u * dh:(u + 1) * dh, W * j:W * (j + 3)]
            q4 = jnp.concatenate(
                [qt_ref[(u * G + g) * dh:(u * G + g + 1) * dh, W * j:W * (j + 1)] for g in range(G)],
                axis=1)
            s = _dot_tn(ktu, q4)
            s = jnp.concatenate(
                [s[:, g * W:(g + 1) * W] - dist * slopes2[u * G + g] for g in range(G)], axis=1)
            snk = sink_ref[u]
            if shift_by_max:
                m = jnp.maximum(jnp.max(s, axis=0, keepdims=True), snk)
                p = jnp.exp2(s - m)
                den = jnp.sum(p, axis=0, keepdims=True) + jnp.exp2(snk - m)
            else:
                p = jnp.exp2(s)
                den = jnp.sum(p, axis=0, keepdims=True) + jnp.exp2(snk)
            o = _dot(vtu, p.astype(BF16)) / den
            for pr in range(G // 2):
                two = jnp.concatenate(
                    [o[:, (2 * pr) * W:(2 * pr + 1) * W], o[:, (2 * pr + 1) * W:(2 * pr + 2) * W]], axis=0)
                col = (u * G + 2 * pr) * dh
                o_ref[W * j:W * (j + 1), col:col + 2 * dh] = two.T.astype(BF16)


def _swa_attn(qt, kt, vt, pos_row_f, pos_col_f, sink2, shift_by_max):
    nq, S = qt.shape
    nk = kt.shape[0]
    W = WINDOW
    tq = min(SWA_TQ, S)
    nsub = tq // W
    nblk = S // W
    cur = lambda i: (0, i)
    prev = lambda i: (0, jnp.maximum(i * nsub - 1, 0))
    nxt = lambda i: (0, jnp.minimum((i + 1) * nsub, nblk - 1))
    cur_c = lambda i: (i, 0)
    prev_c = lambda i: (jnp.maximum(i * nsub - 1, 0), 0)
    nxt_c = lambda i: (jnp.minimum((i + 1) * nsub, nblk - 1), 0)
    return pl.pallas_call(
        functools.partial(_swa_attn_body, seq_len=S, shift_by_max=shift_by_max),
        grid=(S // tq,),
        in_specs=[
            pl.BlockSpec((nq, tq), cur),
            pl.BlockSpec((nk, W), prev), pl.BlockSpec((nk, tq), cur), pl.BlockSpec((nk, W), nxt),
            pl.BlockSpec((nk, W), prev), pl.BlockSpec((nk, tq), cur), pl.BlockSpec((nk, W), nxt),
            pl.BlockSpec((1, tq), cur),
            pl.BlockSpec((W, 1), prev_c), pl.BlockSpec((tq, 1), cur_c), pl.BlockSpec((W, 1), nxt_c),
            pl.BlockSpec(sink2.shape, lambda i: (0, 0, 0)),
        ],
        out_specs=pl.BlockSpec((tq, nq), lambda i: (i, 0)),
        out_shape=jax.ShapeDtypeStruct((S, nq), BF16),
        compiler_params=_cparams("arbitrary"),
        name="swa_attn",
    )(qt, kt, kt, kt, vt, vt, vt, pos_row_f, pos_col_f, pos_col_f, pos_col_f, sink2)


def kernel(x, positions, norm_g, ffn_w_gate, ffn_w_up, ffn_w_down, mla_w_a, mla_q_a_g, mla_kv_a_g,
           mla_w_uq, mla_w_ukv, mla_q_g, mla_k_g, mla_w_o, swa_w_qkv, swa_q_g, swa_k_g, swa_sink, swa_w_o):
    B, S, D = x.shape
    assert B == 1
    depth = norm_g.shape[0]
    H = MLA_HEADS
    half = MLA_ROPE // 2

    pos_row = positions.reshape(1, S)
    pos_col = positions.reshape(S, 1)
    cos_t, sin_t, c2, s2 = _rope_tables(pos_row, pos_col)
    pos_row_f = pos_row.astype(F32)
    pos_col_f = pos_col.astype(F32)

    wg, wu, wd = ffn_w_gate, ffn_w_up, ffn_w_down

    xs = x.reshape(S, D)
    for i in range(depth):
        g = norm_g[i]
        xs = _ffn(xs, g[0], wg, wu, wd, i, 0)
        j = i // 2
        if i % 2 == 0:
            wa = mla_w_a[j]
            pe0 = 2 * MLA_RANK
            wa_ext = jnp.concatenate(
                [wa, wa[:, pe0 + half:pe0 + MLA_ROPE], wa[:, pe0:pe0 + half]], axis=1).astype(BF16)
            wuqt = mla_w_uq[j].T.astype(BF16)
            wukv = mla_w_ukv[j].reshape(MLA_RANK, H, MLA_NOPE + MLA_V)
            wkn = wukv[:, :, :MLA_NOPE].reshape(MLA_RANK, H * MLA_NOPE).astype(BF16)
            wvt = wukv[:, :, MLA_NOPE:].reshape(MLA_RANK, H * MLA_V).T.astype(BF16)
            qscale = LOG2E / math.sqrt(MLA_QK)
            qg_col = (mla_q_g[j] * qscale).reshape(MLA_QK, 1)
            kg = mla_k_g[j]
            kgn = kg[:MLA_NOPE].reshape(1, MLA_NOPE)
            kgr = kg[MLA_NOPE:].reshape(1, MLA_ROPE)
            kgrs = jnp.concatenate([kg[MLA_NOPE + half:], kg[MLA_NOPE:MLA_NOPE + half]]).reshape(1, MLA_ROPE)
            qt, k, vt = _mla_proj(xs, g[1], wa_ext, mla_q_a_g[j].reshape(1, MLA_RANK),
                                  mla_kv_a_g[j].reshape(1, MLA_RANK), wuqt, wkn, wvt,
                                  qg_col, kgn, kgr, kgrs, cos_t, sin_t, c2, s2)
            score_bound = (1.02 * LOG2E * math.sqrt(MLA_QK)
                           * jnp.max(jnp.abs(mla_q_g[j])) * jnp.max(jnp.abs(mla_k_g[j])))
            o = lax.cond(score_bound <= UNSHIFTED_EXP2_LIMIT,
                         lambda: _mla_attn(qt, k, vt, False),
                         lambda: _mla_attn(qt, k, vt, True))
            xs = _oproj(xs, o, mla_w_o[j].astype(BF16))
        else:
            wt = swa_w_qkv[j].T.astype(BF16)
            qscale = LOG2E / math.sqrt(SWA_DH)
            gcol = jnp.concatenate([jnp.tile(swa_q_g[j] * qscale, SWA_Q_HEADS),
                                    jnp.tile(swa_k_g[j], SWA_KV_HEADS)]).reshape(-1, 1)
            qt, kt, vt = _swa_proj(xs, g[1], wt, gcol)
            sink2 = jnp.repeat(swa_sink[j] * LOG2E, WINDOW).reshape(SWA_KV_HEADS, 1, SWA_GROUP * WINDOW)
            score_bound = jnp.maximum(
                1.02 * LOG2E * math.sqrt(SWA_DH) * jnp.max(jnp.abs(swa_q_g[j])) * jnp.max(jnp.abs(swa_k_g[j])),
                jnp.max(jnp.abs(sink2)))
            o = lax.cond(score_bound <= UNSHIFTED_EXP2_LIMIT,
                         lambda: _swa_attn(qt, kt, vt, pos_row_f, pos_col_f, sink2, False),
                         lambda: _swa_attn(qt, kt, vt, pos_row_f, pos_col_f, sink2, True))
            xs = _oproj(xs, o, swa_w_o[j].astype(BF16))
        xs = _ffn(xs, g[2], wg, wu, wd, i, 1)
    return xs.reshape(B, S, D)
```

```python
import functools
import math

import jax
import jax.numpy as jnp
import numpy as np
from jax import lax
from jax.experimental import pallas as pl
from jax.experimental.pallas import tpu as pltpu

F32 = jnp.float32
BF16 = jnp.bfloat16

EPS = 1e-6
NEG_BIG = -1e30
LOG2E = 1.4426950408889634
ROPE_THETA = 10000.0
UNSHIFTED_EXP2_LIMIT = 60.0

MLA_HEADS = 16
MLA_NOPE = 128
MLA_ROPE = 64
MLA_QK = MLA_NOPE + MLA_ROPE
MLA_V = 128
MLA_RANK = 512
SWA_Q_HEADS = 32
SWA_KV_HEADS = 8
SWA_GROUP = SWA_Q_HEADS // SWA_KV_HEADS
SWA_DH = 64
WINDOW = 128

VMEM_LIMIT_BYTES = 58 * 1024 * 1024

FFN_TM = 1024
FFN_TF = 256
FFN_SLOTS = 3
PROJ_TS = 256
ATT_TQ = 1024
ATT_TK = 512
OPROJ_TS = 512
SWA_TQ = 512
ROPE_TS = 1024


def _cparams(*sem):
    return pltpu.CompilerParams(dimension_semantics=sem, vmem_limit_bytes=VMEM_LIMIT_BYTES)


def _dot(a, b):
    return jnp.dot(a, b, preferred_element_type=F32)


def _dot_nt(a, b):
    return lax.dot_general(a, b, (((1,), (1,)), ((), ())), preferred_element_type=F32)


def _dot_tn(a, b):
    return lax.dot_general(a, b, (((0,), (0,)), ((), ())), preferred_element_type=F32)


def _rms_rows(x, g):
    ms = jnp.mean(x * x, axis=-1, keepdims=True)
    return x * lax.rsqrt(ms + EPS) * g


def _rope_body(prow_ref, pcol_ref, fcol_ref, frow_ref, sgn_ref, cost_ref, sint_ref, c2_ref, s2_ref):
    ang_t = fcol_ref[...] * prow_ref[...].astype(F32)
    cost_ref[...] = jnp.cos(ang_t)
    sint_ref[...] = jnp.sin(ang_t)
    ang = pcol_ref[...].astype(F32) * frow_ref[...]
    c2_ref[...] = jnp.cos(ang)
    s2_ref[...] = jnp.sin(ang) * sgn_ref[...]


def _rope_tables(pos_row, pos_col):
    S = pos_row.shape[1]
    half = MLA_ROPE // 2
    inv_freq = 1.0 / (ROPE_THETA ** (jnp.arange(0, MLA_ROPE, 2, dtype=F32) / MLA_ROPE))
    fcol = inv_freq.reshape(half, 1)
    frow = jnp.concatenate([inv_freq, inv_freq]).reshape(1, MLA_ROPE)
    sgn = jnp.concatenate([-jnp.ones((half,), F32), jnp.ones((half,), F32)]).reshape(1, MLA_ROPE)
    ts = min(ROPE_TS, S)
    return pl.pallas_call(
        _rope_body,
        grid=(S // ts,),
        in_specs=[
            pl.BlockSpec((1, ts), lambda i: (0, i)),
            pl.BlockSpec((ts, 1), lambda i: (i, 0)),
            pl.BlockSpec((half, 1), lambda i: (0, 0)),
            pl.BlockSpec((1, MLA_ROPE), lambda i: (0, 0)),
            pl.BlockSpec((1, MLA_ROPE), lambda i: (0, 0)),
        ],
        out_specs=[
            pl.BlockSpec((half, ts), lambda i: (0, i)),
            pl.BlockSpec((half, ts), lambda i: (0, i)),
            pl.BlockSpec((ts, MLA_ROPE), lambda i: (i, 0)),
            pl.BlockSpec((ts, MLA_ROPE), lambda i: (i, 0)),
        ],
        out_shape=[
            jax.ShapeDtypeStruct((half, S), F32),
            jax.ShapeDtypeStruct((half, S), F32),
            jax.ShapeDtypeStruct((S, MLA_ROPE), F32),
            jax.ShapeDtypeStruct((S, MLA_ROPE), F32),
        ],
        compiler_params=_cparams("arbitrary"),
        name="rope_tables",
    )(pos_row, pos_col, fcol, frow, sgn)


def _ffn_body(g_ref, x_hbm, wg_hbm, wu_hbm, wd_hbm, o_ref, xbuf, xn_ref, gbuf, ubuf, dbuf, wsem, xsem,
              *, layer, which, tf, n_chunks):
    i = pl.program_id(0)
    n_tiles = pl.num_programs(0)
    tm = xbuf.shape[0]
    total = n_tiles * n_chunks
    ahead = FFN_SLOTS - 1

    def weight_copies(q):
        slot = lax.rem(q, FFN_SLOTS)
        cols = pl.ds(pl.multiple_of(lax.rem(q, n_chunks) * tf, tf), tf)
        return (pltpu.make_async_copy(wg_hbm.at[layer, which, :, cols], gbuf.at[slot], wsem.at[0, slot]),
                pltpu.make_async_copy(wu_hbm.at[layer, which, :, cols], ubuf.at[slot], wsem.at[1, slot]),
                pltpu.make_async_copy(wd_hbm.at[layer, which, cols, :], dbuf.at[slot], wsem.at[2, slot]))

    def x_copy(tile):
        rows = pl.ds(pl.multiple_of(tile * tm, tm), tm)
        return pltpu.make_async_copy(x_hbm.at[rows, :], xbuf, xsem.at[0])

    @pl.when(i == 0)
    def _():
        x_copy(0).start()
        for q in range(ahead):
            for cp in weight_copies(q):
                cp.start()

    x_copy(i).wait()
    x = xbuf[...]
    xn_ref[...] = _rms_rows(x, g_ref[...]).astype(BF16)
    o_ref[...] = x

    @pl.when(i + 1 < n_tiles)
    def _():
        x_copy(i + 1).start()

    def chunk(c, carry):
        q = i * n_chunks + c
        for cp in weight_copies(q):
            cp.wait()

        @pl.when(q + ahead < total)
        def _():
            for cp in weight_copies(q + ahead):
                cp.start()

        slot = lax.rem(q, FFN_SLOTS)
        xn = xn_ref[...]
        gate = _dot(xn, gbuf[slot].astype(BF16))
        up = _dot(xn, ubuf[slot].astype(BF16))
        h = (gate * (0.5 / (1.0 + jnp.exp(-gate))) * up).astype(BF16)
        o_ref[...] += _dot(h, dbuf[slot].astype(BF16))
        return carry

    lax.fori_loop(0, n_chunks, chunk, 0, unroll=2)


def _ffn(x, g, wg, wu, wd, layer, which):
    S, D = x.shape
    F = wg.shape[3]
    tm = min(FFN_TM, S)
    tf = FFN_TF
    n_chunks = F // tf
    assert n_chunks * tf == F and n_chunks >= FFN_SLOTS and S % tm == 0
    return pl.pallas_call(
        functools.partial(_ffn_body, layer=layer, which=which, tf=tf, n_chunks=n_chunks),
        grid=(S // tm,),
        in_specs=[
            pl.BlockSpec((1, D), lambda i: (0, 0)),
            pl.BlockSpec(memory_space=pl.ANY),
            pl.BlockSpec(memory_space=pl.ANY),
            pl.BlockSpec(memory_space=pl.ANY),
            pl.BlockSpec(memory_space=pl.ANY),
        ],
        out_specs=pl.BlockSpec((tm, D), lambda i: (i, 0)),
        out_shape=jax.ShapeDtypeStruct((S, D), F32),
        scratch_shapes=[
            pltpu.VMEM((tm, D), F32),
            pltpu.VMEM((tm, D), BF16),
            pltpu.VMEM((FFN_SLOTS, D, tf), F32),
            pltpu.VMEM((FFN_SLOTS, D, tf), F32),
            pltpu.VMEM((FFN_SLOTS, tf, D), F32),
            pltpu.SemaphoreType.DMA((3, FFN_SLOTS)),
            pltpu.SemaphoreType.DMA((1,)),
        ],
        compiler_params=_cparams("arbitrary"),
        name="ffn",
    )(g.reshape(1, D), x, wg, wu, wd)


def _mla_proj_body(x_ref, g_ref, wa_ref, qag_ref, kvag_ref, wuqt_ref, wkn_ref, wvt_ref,
                   qg_ref, kgn_ref, kgr_ref, kgrs_ref, cost_ref, sint_ref, c2_ref, s2_ref,
                   qt_ref, k_ref, vt_ref):
    H = MLA_HEADS
    h = _rms_rows(x_ref[...], g_ref[...]).astype(BF16)
    a = _dot(h, wa_ref[...])
    cq = _rms_rows(a[:, :MLA_RANK], qag_ref[...]).astype(BF16)
    ckv = _rms_rows(a[:, MLA_RANK:2 * MLA_RANK], kvag_ref[...]).astype(BF16)
    kpe = a[:, 2 * MLA_RANK:2 * MLA_RANK + MLA_ROPE]
    kpe_sw = a[:, 2 * MLA_RANK + MLA_ROPE:]
    ss_pe = jnp.sum(kpe * kpe, axis=1, keepdims=True)
    rk = kpe * kgr_ref[...] * c2_ref[...] + kpe_sw * kgrs_ref[...] * s2_ref[...]

    qt = _dot_nt(wuqt_ref[...], cq)
    cos_t = cost_ref[...]
    sin_t = sint_ref[...]
    qg = qg_ref[...]
    half = MLA_ROPE // 2
    for hd in range(H):
        blk = qt[hd * MLA_QK:(hd + 1) * MLA_QK]
        ms = jnp.sum(blk * blk, axis=0, keepdims=True) * (1.0 / MLA_QK)
        y = blk * lax.rsqrt(ms + EPS) * qg
        y1 = y[MLA_NOPE:MLA_NOPE + half]
        y2 = y[MLA_NOPE + half:]
        qt_ref[hd * MLA_QK:hd * MLA_QK + MLA_NOPE, :] = y[:MLA_NOPE].astype(BF16)
        qt_ref[hd * MLA_QK + MLA_NOPE:hd * MLA_QK + MLA_NOPE + half, :] = (
            y1 * cos_t - y2 * sin_t).astype(BF16)
        qt_ref[hd * MLA_QK + MLA_NOPE + half:(hd + 1) * MLA_QK, :] = (
            y2 * cos_t + y1 * sin_t).astype(BF16)

    kn = _dot(ckv, wkn_ref[...])
    kgn = kgn_ref[...]
    for hd in range(H):
        kh = kn[:, hd * MLA_NOPE:(hd + 1) * MLA_NOPE]
        ms = (jnp.sum(kh * kh, axis=1, keepdims=True) + ss_pe) * (1.0 / MLA_QK)
        r = lax.rsqrt(ms + EPS)
        k_ref[hd, :, :MLA_NOPE] = (kh * r * kgn).astype(BF16)
        k_ref[hd, :, MLA_NOPE:] = (rk * r).astype(BF16)

    vt = _dot_nt(wvt_ref[...], ckv).astype(BF16)
    for hd in range(H):
        vt_ref[hd, 0] = vt[hd * MLA_V:(hd + 1) * MLA_V]


def _stacked(w, layer):
    return pl.BlockSpec((None,) + w.shape[1:], lambda i: (layer, 0, 0))


def _mla_proj(x, g, wa_ext, qag, kvag, wuqt, wkn, wvt, qg_col, kgn, kgr, kgrs, cos_t, sin_t, c2, s2, layer):
    S, D = x.shape
    H = MLA_HEADS
    ts = min(PROJ_TS, S)
    nc = S // ts
    half = MLA_ROPE // 2
    const = lambda i: (0, 0)
    return pl.pallas_call(
        _mla_proj_body,
        grid=(nc,),
        in_specs=[
            pl.BlockSpec((ts, D), lambda i: (i, 0)),
            pl.BlockSpec((1, D), const),
            _stacked(wa_ext, layer),
            pl.BlockSpec((1, MLA_RANK), const),
            pl.BlockSpec((1, MLA_RANK), const),
            _stacked(wuqt, layer),
            _stacked(wkn, layer),
            _stacked(wvt, layer),
            pl.BlockSpec((MLA_QK, 1), const),
            pl.BlockSpec((1, MLA_NOPE), const),
            pl.BlockSpec((1, MLA_ROPE), const),
            pl.BlockSpec((1, MLA_ROPE), const),
            pl.BlockSpec((half, ts), lambda i: (0, i)),
            pl.BlockSpec((half, ts), lambda i: (0, i)),
            pl.BlockSpec((ts, MLA_ROPE), lambda i: (i, 0)),
            pl.BlockSpec((ts, MLA_ROPE), lambda i: (i, 0)),
        ],
        out_specs=[
            pl.BlockSpec((H * MLA_QK, ts), lambda i: (0, i)),
            pl.BlockSpec((H, ts, MLA_QK), lambda i: (0, i, 0)),
            pl.BlockSpec((H, 1, MLA_V, ts), lambda i: (0, i, 0, 0)),
        ],
        out_shape=[
            jax.ShapeDtypeStruct((H * MLA_QK, S), BF16),
            jax.ShapeDtypeStruct((H, S, MLA_QK), BF16),
            jax.ShapeDtypeStruct((H, nc, MLA_V, ts), BF16),
        ],
        compiler_params=_cparams("arbitrary"),
        name="mla_proj",
    )(x, g.reshape(1, D), wa_ext, qag, kvag, wuqt, wkn, wvt, qg_col, kgn, kgr, kgrs, cos_t, sin_t, c2, s2)


def _mla_attn_body(qt_ref, k_ref, vt_ref, o_ref, *, tk, tc, shift_by_max):
    qt = qt_ref[...]
    tq = qt.shape[1]
    n_chunks = k_ref.shape[0] // tk
    per = tk // tc

    def scores(c):
        return _dot(k_ref[c * tk:(c + 1) * tk, :], qt)

    def weighted_values(c, pb):
        pv = _dot(vt_ref[c * per], pb[:tc])
        for u in range(1, per):
            pv += _dot(vt_ref[c * per + u], pb[u * tc:(u + 1) * tc])
        return pv

    m = jnp.full((1, tq), NEG_BIG, F32)
    l = jnp.zeros((1, tq), F32)
    acc = jnp.zeros((MLA_V, tq), F32)
    s_next = scores(0)
    pending = None
    for c in range(n_chunks + 1):
        s = s_next
        if c + 1 < n_chunks:
            s_next = scores(c + 1)
        if pending is not None:
            alpha, pb = pending
            pv = weighted_values(c - 1, pb)
            acc = acc + pv if alpha is None else alpha * acc + pv
        if c == n_chunks:
            break
        if shift_by_max:
            m_new = jnp.maximum(m, jnp.max(s, axis=0, keepdims=True))
            alpha = jnp.exp2(m - m_new)
            p = jnp.exp2(s - m_new)
            l = alpha * l + jnp.sum(p, axis=0, keepdims=True)
            m = m_new
        else:
            alpha = None
            p = jnp.exp2(s)
            l = l + jnp.sum(p, axis=0, keepdims=True)
        pending = (alpha, p.astype(BF16))
    o_ref[...] = (acc / l).T.astype(BF16)


def _mla_attn(qt, k, vt, shift_by_max):
    H, S, _ = k.shape
    nc, tc = vt.shape[1], vt.shape[3]
    tq = min(ATT_TQ, S)
    tk = min(ATT_TK, S)
    return pl.pallas_call(
        functools.partial(_mla_attn_body, tk=tk, tc=tc, shift_by_max=shift_by_max),
        grid=(H, S // tq),
        in_specs=[
            pl.BlockSpec((MLA_QK, tq), lambda h, i: (h, i)),
            pl.BlockSpec((None, S, MLA_QK), lambda h, i: (h, 0, 0)),
            pl.BlockSpec((None, nc, MLA_V, tc), lambda h, i: (h, 0, 0, 0)),
        ],
        out_specs=pl.BlockSpec((tq, MLA_V), lambda h, i: (i, h)),
        out_shape=jax.ShapeDtypeStruct((S, H * MLA_V), BF16),
        compiler_params=_cparams("arbitrary", "arbitrary"),
        name="mla_attn",
    )(qt, k, vt)


def _oproj_body(x_ref, o_ref, w_ref, y_ref):
    y_ref[...] = x_ref[...] + _dot(o_ref[...], w_ref[...])


def _oproj(x, o, w, layer):
    S, D = x.shape
    ts = min(OPROJ_TS, S)
    return pl.pallas_call(
        _oproj_body,
        grid=(S // ts,),
        in_specs=[
            pl.BlockSpec((ts, D), lambda i: (i, 0)),
            pl.BlockSpec((ts, o.shape[1]), lambda i: (i, 0)),
            _stacked(w, layer),
        ],
        out_specs=pl.BlockSpec((ts, D), lambda i: (i, 0)),
        out_shape=jax.ShapeDtypeStruct((S, D), F32),
        compiler_params=_cparams("arbitrary"),
        name="oproj",
    )(x, o, w)


def _swa_proj_body(x_ref, g_ref, wt_ref, gcol_ref, qt_ref, kt_ref, vt_ref):
    nq = SWA_Q_HEADS * SWA_DH
    nk = SWA_KV_HEADS * SWA_DH
    nh = SWA_Q_HEADS + SWA_KV_HEADS
    h = _rms_rows(x_ref[...], g_ref[...]).astype(BF16)
    qkv = _dot_nt(wt_ref[...], h)
    ts = qkv.shape[1]
    qk = qkv[:nq + nk].reshape(nh, SWA_DH, ts)
    ms = jnp.sum(qk * qk, axis=1, keepdims=True) * (1.0 / SWA_DH)
    y = (qk * lax.rsqrt(ms + EPS)).reshape(nq + nk, ts) * gcol_ref[...]
    qt_ref[...] = y[:nq].astype(BF16)
    kt_ref[...] = y[nq:].astype(BF16)
    vt_ref[...] = qkv[nq + nk:].astype(BF16)


def _swa_proj(x, g, wt, gcol, layer):
    S, D = x.shape
    nq = SWA_Q_HEADS * SWA_DH
    nk = SWA_KV_HEADS * SWA_DH
    ts = min(PROJ_TS, S)
    const = lambda i: (0, 0)
    return pl.pallas_call(
        _swa_proj_body,
        grid=(S // ts,),
        in_specs=[
            pl.BlockSpec((ts, D), lambda i: (i, 0)),
            pl.BlockSpec((1, D), const),
            _stacked(wt, layer),
            pl.BlockSpec((nq + nk, 1), const),
        ],
        out_specs=[
            pl.BlockSpec((nq, ts), lambda i: (0, i)),
            pl.BlockSpec((nk, ts), lambda i: (0, i)),
            pl.BlockSpec((nk, ts), lambda i: (0, i)),
        ],
        out_shape=[
            jax.ShapeDtypeStruct((nq, S), BF16),
            jax.ShapeDtypeStruct((nk, S), BF16),
            jax.ShapeDtypeStruct((nk, S), BF16),
        ],
        compiler_params=_cparams("arbitrary"),
        name="swa_proj",
    )(x, g.reshape(1, D), wt, gcol)


def _alibi_slopes2():
    n = SWA_Q_HEADS
    s = (2.0 ** (-8.0 * np.arange(1, n + 1) / n)).astype(np.float32)
    return [float(v) * LOG2E for v in s]


def _swa_attn_body(qt_ref, ktp_ref, ktc_ref, ktn_ref, vtp_ref, vtc_ref, vtn_ref,
                   pqr_ref, pkp_ref, pkc_ref, pkn_ref, sink_ref, o_ref, *, seq_len, shift_by_max):
    W = WINDOW
    G = SWA_GROUP
    dh = SWA_DH
    tq = qt_ref.shape[1]
    nsub = tq // W
    t0 = pl.program_id(0) * tq
    slopes2 = _alibi_slopes2()

    kt_all = jnp.concatenate([ktp_ref[...], ktc_ref[...], ktn_ref[...]], axis=1)
    vt_all = jnp.concatenate([vtp_ref[...], vtc_ref[...], vtn_ref[...]], axis=1)
    pk_all = jnp.concatenate([pkp_ref[...], pkc_ref[...], pkn_ref[...]], axis=0)
    pq_all = pqr_ref[...]

    for j in range(nsub):
        kidx = t0 + W * (j - 1) + lax.broadcasted_iota(jnp.int32, (3 * W, W), 0)
        qidx = t0 + W * j + lax.broadcasted_iota(jnp.int32, (3 * W, W), 1)
        rel = kidx - qidx
        ok = (rel >= -W) & (rel <= W) & (kidx >= 0) & (kidx < seq_len)
        dist = jnp.abs(pk_all[W * j:W * (j + 3)] - pq_all[:, W * j:W * (j + 1)])
        dist = jnp.where(ok, dist, -NEG_BIG)
        for u in range(SWA_KV_HEADS):
            ktu = kt_all[u * dh:(u + 1) * dh, W * j:W * (j + 3)]
            vtu = vt_all[u * dh:(u + 1) * dh, W * j:W * (j + 3)]
            q4 = jnp.concatenate(
                [qt_ref[(u * G + g) * dh:(u * G + g + 1) * dh, W * j:W * (j + 1)] for g in range(G)],
                axis=1)
            s = _dot_tn(ktu, q4)
            s = jnp.concatenate(
                [s[:, g * W:(g + 1) * W] - dist * slopes2[u * G + g] for g in range(G)], axis=1)
            snk = sink_ref[u]
            if shift_by_max:
                m = jnp.maximum(jnp.max(s, axis=0, keepdims=True), snk)
                p = jnp.exp2(s - m)
                den = jnp.sum(p, axis=0, keepdims=True) + jnp.exp2(snk - m)
            else:
                p = jnp.exp2(s)
                den = jnp.sum(p, axis=0, keepdims=True) + jnp.exp2(snk)
            o = _dot(vtu, p.astype(BF16)) / den
            for pr in range(G // 2):
                two = jnp.concatenate(
                    [o[:, (2 * pr) * W:(2 * pr + 1) * W], o[:, (2 * pr + 1) * W:(2 * pr + 2) * W]], axis=0)
                col = (u * G + 2 * pr) * dh
                o_ref[W * j:W * (j + 1), col:col + 2 * dh] = two.T.astype(BF16)


def _swa_attn(qt, kt, vt, pos_row_f, pos_col_f, sink2, shift_by_max):
    nq, S = qt.shape
    nk = kt.shape[0]
    W = WINDOW
    tq = min(SWA_TQ, S)
    nsub = tq // W
    nblk = S // W
    cur = lambda i: (0, i)
    prev = lambda i: (0, jnp.maximum(i * nsub - 1, 0))
    nxt = lambda i: (0, jnp.minimum((i + 1) * nsub, nblk - 1))
    cur_c = lambda i: (i, 0)
    prev_c = lambda i: (jnp.maximum(i * nsub - 1, 0), 0)
    nxt_c = lambda i: (jnp.minimum((i + 1) * nsub, nblk - 1), 0)
    return pl.pallas_call(
        functools.partial(_swa_attn_body, seq_len=S, shift_by_max=shift_by_max),
        grid=(S // tq,),
        in_specs=[
            pl.BlockSpec((nq, tq), cur),
            pl.BlockSpec((nk, W), prev), pl.BlockSpec((nk, tq), cur), pl.BlockSpec((nk, W), nxt),
            pl.BlockSpec((nk, W), prev), pl.BlockSpec((nk, tq), cur), pl.BlockSpec((nk, W), nxt),
            pl.BlockSpec((1, tq), cur),
            pl.BlockSpec((W, 1), prev_c), pl.BlockSpec((tq, 1), cur_c), pl.BlockSpec((W, 1), nxt_c),
            pl.BlockSpec(sink2.shape, lambda i: (0, 0, 0)),
        ],
        out_specs=pl.BlockSpec((tq, nq), lambda i: (i, 0)),
        out_shape=jax.ShapeDtypeStruct((S, nq), BF16),
        compiler_params=_cparams("arbitrary"),
        name="swa_attn",
    )(qt, kt, kt, kt, vt, vt, vt, pos_row_f, pos_col_f, pos_col_f, pos_col_f, sink2)


def kernel(x, positions, norm_g, ffn_w_gate, ffn_w_up, ffn_w_down, mla_w_a, mla_q_a_g, mla_kv_a_g,
           mla_w_uq, mla_w_ukv, mla_q_g, mla_k_g, mla_w_o, swa_w_qkv, swa_q_g, swa_k_g, swa_sink, swa_w_o):
    B, S, D = x.shape
    assert B == 1
    depth = norm_g.shape[0]
    H = MLA_HEADS
    half = MLA_ROPE // 2

    pos_row = positions.reshape(1, S)
    pos_col = positions.reshape(S, 1)
    cos_t, sin_t, c2, s2 = _rope_tables(pos_row, pos_col)
    pos_row_f = pos_row.astype(F32)
    pos_col_f = pos_col.astype(F32)

    wg, wu, wd = ffn_w_gate, ffn_w_up, ffn_w_down

    n_mla = mla_w_a.shape[0]
    pe0 = 2 * MLA_RANK
    wa_ext = jnp.concatenate(
        [mla_w_a, mla_w_a[:, :, pe0 + half:pe0 + MLA_ROPE], mla_w_a[:, :, pe0:pe0 + half]], axis=2).astype(BF16)
    wuqt = jnp.swapaxes(mla_w_uq, 1, 2).astype(BF16)
    wukv = mla_w_ukv.reshape(n_mla, MLA_RANK, H, MLA_NOPE + MLA_V)
    wkn = wukv[:, :, :, :MLA_NOPE].reshape(n_mla, MLA_RANK, H * MLA_NOPE).astype(BF16)
    wvt = jnp.swapaxes(wukv[:, :, :, MLA_NOPE:].reshape(n_mla, MLA_RANK, H * MLA_V), 1, 2).astype(BF16)
    mla_wo = mla_w_o.astype(BF16)
    swa_wt = jnp.swapaxes(swa_w_qkv, 1, 2).astype(BF16)
    swa_wo = swa_w_o.astype(BF16)

    xs = x.reshape(S, D)
    for i in range(depth):
        g = norm_g[i]
        xs = _ffn(xs, g[0], wg, wu, wd, i, 0)
        j = i // 2
        if i % 2 == 0:
            qscale = LOG2E / math.sqrt(MLA_QK)
            qg_col = (mla_q_g[j] * qscale).reshape(MLA_QK, 1)
            kg = mla_k_g[j]
            kgn = kg[:MLA_NOPE].reshape(1, MLA_NOPE)
            kgr = kg[MLA_NOPE:].reshape(1, MLA_ROPE)
            kgrs = jnp.concatenate([kg[MLA_NOPE + half:], kg[MLA_NOPE:MLA_NOPE + half]]).reshape(1, MLA_ROPE)
            qt, k, vt = _mla_proj(xs, g[1], wa_ext, mla_q_a_g[j].reshape(1, MLA_RANK),
                                  mla_kv_a_g[j].reshape(1, MLA_RANK), wuqt, wkn, wvt,
                                  qg_col, kgn, kgr, kgrs, cos_t, sin_t, c2, s2, j)
            score_bound = (1.02 * LOG2E * math.sqrt(MLA_QK)
                           * jnp.max(jnp.abs(mla_q_g[j])) * jnp.max(jnp.abs(mla_k_g[j])))
            o = lax.cond(score_bound <= UNSHIFTED_EXP2_LIMIT,
                         lambda: _mla_attn(qt, k, vt, False),
                         lambda: _mla_attn(qt, k, vt, True))
            xs = _oproj(xs, o, mla_wo, j)
        else:
            qscale = LOG2E / math.sqrt(SWA_DH)
            gcol = jnp.concatenate([jnp.tile(swa_q_g[j] * qscale, SWA_Q_HEADS),
                                    jnp.tile(swa_k_g[j], SWA_KV_HEADS)]).reshape(-1, 1)
            qt, kt, vt = _swa_proj(xs, g[1], swa_wt, gcol, j)
            sink2 = jnp.repeat(swa_sink[j] * LOG2E, WINDOW).reshape(SWA_KV_HEADS, 1, SWA_GROUP * WINDOW)
            score_bound = jnp.maximum(
                1.02 * LOG2E * math.sqrt(SWA_DH) * jnp.max(jnp.abs(swa_q_g[j])) * jnp.max(jnp.abs(swa_k_g[j])),
                jnp.max(jnp.abs(sink2)))
            o = lax.cond(score_bound <= UNSHIFTED_EXP2_LIMIT,
                         lambda: _swa_attn(qt, kt, vt, pos_row_f, pos_col_f, sink2, False),
                         lambda: _swa_attn(qt, kt, vt, pos_row_f, pos_col_f, sink2, True))
            xs = _oproj(xs, o, swa_wo, j)
        xs = _ffn(xs, g[2], wg, wu, wd, i, 1)
    return xs.reshape(B, S, D)
```

```python
import functools
import math

import jax
import jax.numpy as jnp
import numpy as np
from jax import lax
from jax.experimental import pallas as pl
from jax.experimental.pallas import tpu as pltpu

F32 = jnp.float32
BF16 = jnp.bfloat16

EPS = 1e-6
NEG_BIG = -1e30
LOG2E = 1.4426950408889634
ROPE_THETA = 10000.0
UNSHIFTED_EXP2_LIMIT = 60.0

MLA_HEADS = 16
MLA_NOPE = 128
MLA_ROPE = 64
MLA_QK = MLA_NOPE + MLA_ROPE
MLA_V = 128
MLA_RANK = 512
SWA_Q_HEADS = 32
SWA_KV_HEADS = 8
SWA_GROUP = SWA_Q_HEADS // SWA_KV_HEADS
SWA_DH = 64
WINDOW = 128

VMEM_LIMIT_BYTES = 58 * 1024 * 1024

FFN_TM = 1024
FFN_TF = 256
FFN_SLOTS = 4
PROJ_TS = 256
ATT_TQ = 1024
ATT_TK = 512
OPROJ_TS = 512
SWA_TQ = 512
ROPE_TS = 1024


def _cparams(*sem):
    return pltpu.CompilerParams(dimension_semantics=sem, vmem_limit_bytes=VMEM_LIMIT_BYTES)


def _dot(a, b):
    return jnp.dot(a, b, preferred_element_type=F32)


def _dot_nt(a, b):
    return lax.dot_general(a, b, (((1,), (1,)), ((), ())), preferred_element_type=F32)


def _dot_tn(a, b):
    return lax.dot_general(a, b, (((0,), (0,)), ((), ())), preferred_element_type=F32)


def _rms_rows(x, g):
    ms = jnp.mean(x * x, axis=-1, keepdims=True)
    return x * lax.rsqrt(ms + EPS) * g


def _rope_body(prow_ref, pcol_ref, fcol_ref, frow_ref, sgn_ref, cost_ref, sint_ref, c2_ref, s2_ref):
    ang_t = fcol_ref[...] * prow_ref[...].astype(F32)
    cost_ref[...] = jnp.cos(ang_t)
    sint_ref[...] = jnp.sin(ang_t)
    ang = pcol_ref[...].astype(F32) * frow_ref[...]
    c2_ref[...] = jnp.cos(ang)
    s2_ref[...] = jnp.sin(ang) * sgn_ref[...]


def _rope_tables(pos_row, pos_col):
    S = pos_row.shape[1]
    half = MLA_ROPE // 2
    inv_freq = 1.0 / (ROPE_THETA ** (jnp.arange(0, MLA_ROPE, 2, dtype=F32) / MLA_ROPE))
    fcol = inv_freq.reshape(half, 1)
    frow = jnp.concatenate([inv_freq, inv_freq]).reshape(1, MLA_ROPE)
    sgn = jnp.concatenate([-jnp.ones((half,), F32), jnp.ones((half,), F32)]).reshape(1, MLA_ROPE)
    ts = min(ROPE_TS, S)
    return pl.pallas_call(
        _rope_body,
        grid=(S // ts,),
        in_specs=[
            pl.BlockSpec((1, ts), lambda i: (0, i)),
            pl.BlockSpec((ts, 1), lambda i: (i, 0)),
            pl.BlockSpec((half, 1), lambda i: (0, 0)),
            pl.BlockSpec((1, MLA_ROPE), lambda i: (0, 0)),
            pl.BlockSpec((1, MLA_ROPE), lambda i: (0, 0)),
        ],
        out_specs=[
            pl.BlockSpec((half, ts), lambda i: (0, i)),
            pl.BlockSpec((half, ts), lambda i: (0, i)),
            pl.BlockSpec((ts, MLA_ROPE), lambda i: (i, 0)),
            pl.BlockSpec((ts, MLA_ROPE), lambda i: (i, 0)),
        ],
        out_shape=[
            jax.ShapeDtypeStruct((half, S), F32),
            jax.ShapeDtypeStruct((half, S), F32),
            jax.ShapeDtypeStruct((S, MLA_ROPE), F32),
            jax.ShapeDtypeStruct((S, MLA_ROPE), F32),
        ],
        compiler_params=_cparams("arbitrary"),
        name="rope_tables",
    )(pos_row, pos_col, fcol, frow, sgn)


def _ffn_body(g_ref, x_hbm, wg_hbm, wu_hbm, wd_hbm, o_ref, xbuf, xn_ref, gbuf, ubuf, dbuf, wsem, xsem,
              *, layer, which, tf, n_chunks):
    i = pl.program_id(0)
    n_tiles = pl.num_programs(0)
    tm = xbuf.shape[0]
    n_pairs = n_chunks // 2
    total_pairs = n_tiles * n_pairs

    def weight_copies(p):
        base = lax.rem(p, 2) * 2
        c0 = lax.rem(p, n_pairs) * 2
        out = []
        for k in range(2):
            slot = base + k
            cols = pl.ds(pl.multiple_of((c0 + k) * tf, tf), tf)
            out += [pltpu.make_async_copy(wg_hbm.at[layer, which, :, cols], gbuf.at[slot], wsem.at[0, slot]),
                    pltpu.make_async_copy(wu_hbm.at[layer, which, :, cols], ubuf.at[slot], wsem.at[1, slot]),
                    pltpu.make_async_copy(wd_hbm.at[layer, which, cols, :], dbuf.at[slot], wsem.at[2, slot])]
        return out

    def x_copy(tile):
        rows = pl.ds(pl.multiple_of(tile * tm, tm), tm)
        return pltpu.make_async_copy(x_hbm.at[rows, :], xbuf, xsem.at[0])

    @pl.when(i == 0)
    def _():
        x_copy(0).start()
        for cp in weight_copies(0):
            cp.start()

    x_copy(i).wait()
    x = xbuf[...]
    xn_ref[...] = _rms_rows(x, g_ref[...]).astype(BF16)
    o_ref[...] = x

    @pl.when(i + 1 < n_tiles)
    def _():
        x_copy(i + 1).start()

    def compute(slot):
        xn = xn_ref[...]
        gate = _dot(xn, gbuf[slot].astype(BF16))
        up = _dot(xn, ubuf[slot].astype(BF16))
        h = (gate * (0.5 / (1.0 + jnp.exp(-gate))) * up).astype(BF16)
        o_ref[...] += _dot(h, dbuf[slot].astype(BF16))

    def pair(j, carry):
        p = i * n_pairs + j
        for cp in weight_copies(p):
            cp.wait()

        @pl.when(p + 1 < total_pairs)
        def _():
            for cp in weight_copies(p + 1):
                cp.start()

        base = lax.rem(p, 2) * 2
        compute(base)
        compute(base + 1)
        return carry

    lax.fori_loop(0, n_pairs, pair, 0)


def _ffn(x, g, wg, wu, wd, layer, which):
    S, D = x.shape
    F = wg.shape[3]
    tm = min(FFN_TM, S)
    tf = FFN_TF
    n_chunks = F // tf
    assert n_chunks * tf == F and n_chunks % 2 == 0 and FFN_SLOTS == 4 and S % tm == 0
    return pl.pallas_call(
        functools.partial(_ffn_body, layer=layer, which=which, tf=tf, n_chunks=n_chunks),
        grid=(S // tm,),
        in_specs=[
            pl.BlockSpec((1, D), lambda i: (0, 0)),
            pl.BlockSpec(memory_space=pl.ANY),
            pl.BlockSpec(memory_space=pl.ANY),
            pl.BlockSpec(memory_space=pl.ANY),
            pl.BlockSpec(memory_space=pl.ANY),
        ],
        out_specs=pl.BlockSpec((tm, D), lambda i: (i, 0)),
        out_shape=jax.ShapeDtypeStruct((S, D), F32),
        scratch_shapes=[
            pltpu.VMEM((tm, D), F32),
            pltpu.VMEM((tm, D), BF16),
            pltpu.VMEM((FFN_SLOTS, D, tf), F32),
            pltpu.VMEM((FFN_SLOTS, D, tf), F32),
            pltpu.VMEM((FFN_SLOTS, tf, D), F32),
            pltpu.SemaphoreType.DMA((3, FFN_SLOTS)),
            pltpu.SemaphoreType.DMA((1,)),
        ],
        compiler_params=_cparams("arbitrary"),
        name="ffn",
    )(g.reshape(1, D), x, wg, wu, wd)


def _mla_proj_body(x_ref, g_ref, wa_ref, qag_ref, kvag_ref, wuqt_ref, wkn_ref, wvt_ref,
                   qg_ref, kgn_ref, kgr_ref, kgrs_ref, cost_ref, sint_ref, c2_ref, s2_ref,
                   qt_ref, k_ref, vt_ref):
    H = MLA_HEADS
    h = _rms_rows(x_ref[...], g_ref[...]).astype(BF16)
    a = _dot(h, wa_ref[...])
    cq = _rms_rows(a[:, :MLA_RANK], qag_ref[...]).astype(BF16)
    ckv = _rms_rows(a[:, MLA_RANK:2 * MLA_RANK], kvag_ref[...]).astype(BF16)
    kpe = a[:, 2 * MLA_RANK:2 * MLA_RANK + MLA_ROPE]
    kpe_sw = a[:, 2 * MLA_RANK + MLA_ROPE:]
    ss_pe = jnp.sum(kpe * kpe, axis=1, keepdims=True)
    rk = kpe * kgr_ref[...] * c2_ref[...] + kpe_sw * kgrs_ref[...] * s2_ref[...]

    qt = _dot_nt(wuqt_ref[...], cq)
    cos_t = cost_ref[...]
    sin_t = sint_ref[...]
    qg = qg_ref[...]
    half = MLA_ROPE // 2
    for hd in range(H):
        blk = qt[hd * MLA_QK:(hd + 1) * MLA_QK]
        ms = jnp.sum(blk * blk, axis=0, keepdims=True) * (1.0 / MLA_QK)
        y = blk * lax.rsqrt(ms + EPS) * qg
        y1 = y[MLA_NOPE:MLA_NOPE + half]
        y2 = y[MLA_NOPE + half:]
        qt_ref[hd * MLA_QK:hd * MLA_QK + MLA_NOPE, :] = y[:MLA_NOPE].astype(BF16)
        qt_ref[hd * MLA_QK + MLA_NOPE:hd * MLA_QK + MLA_NOPE + half, :] = (
            y1 * cos_t - y2 * sin_t).astype(BF16)
        qt_ref[hd * MLA_QK + MLA_NOPE + half:(hd + 1) * MLA_QK, :] = (
            y2 * cos_t + y1 * sin_t).astype(BF16)

    kn = _dot(ckv, wkn_ref[...])
    kgn = kgn_ref[...]
    for hd in range(H):
        kh = kn[:, hd * MLA_NOPE:(hd + 1) * MLA_NOPE]
        ms = (jnp.sum(kh * kh, axis=1, keepdims=True) + ss_pe) * (1.0 / MLA_QK)
        r = lax.rsqrt(ms + EPS)
        k_ref[hd, :, :MLA_NOPE] = (kh * r * kgn).astype(BF16)
        k_ref[hd, :, MLA_NOPE:] = (rk * r).astype(BF16)

    vt = _dot_nt(wvt_ref[...], ckv).astype(BF16)
    for hd in range(H):
        vt_ref[hd, 0] = vt[hd * MLA_V:(hd + 1) * MLA_V]


def _stacked(w, layer):
    return pl.BlockSpec((None,) + w.shape[1:], lambda i: (layer, 0, 0))


def _mla_proj(x, g, wa_ext, qag, kvag, wuqt, wkn, wvt, qg_col, kgn, kgr, kgrs, cos_t, sin_t, c2, s2, layer):
    S, D = x.shape
    H = MLA_HEADS
    ts = min(PROJ_TS, S)
    nc = S // ts
    half = MLA_ROPE // 2
    const = lambda i: (0, 0)
    return pl.pallas_call(
        _mla_proj_body,
        grid=(nc,),
        in_specs=[
            pl.BlockSpec((ts, D), lambda i: (i, 0)),
            pl.BlockSpec((1, D), const),
            _stacked(wa_ext, layer),
            pl.BlockSpec((1, MLA_RANK), const),
            pl.BlockSpec((1, MLA_RANK), const),
            _stacked(wuqt, layer),
            _stacked(wkn, layer),
            _stacked(wvt, layer),
            pl.BlockSpec((MLA_QK, 1), const),
            pl.BlockSpec((1, MLA_NOPE), const),
            pl.BlockSpec((1, MLA_ROPE), const),
            pl.BlockSpec((1, MLA_ROPE), const),
            pl.BlockSpec((half, ts), lambda i: (0, i)),
            pl.BlockSpec((half, ts), lambda i: (0, i)),
            pl.BlockSpec((ts, MLA_ROPE), lambda i: (i, 0)),
            pl.BlockSpec((ts, MLA_ROPE), lambda i: (i, 0)),
        ],
        out_specs=[
            pl.BlockSpec((H * MLA_QK, ts), lambda i: (0, i)),
            pl.BlockSpec((H, ts, MLA_QK), lambda i: (0, i, 0)),
            pl.BlockSpec((H, 1, MLA_V, ts), lambda i: (0, i, 0, 0)),
        ],
        out_shape=[
            jax.ShapeDtypeStruct((H * MLA_QK, S), BF16),
            jax.ShapeDtypeStruct((H, S, MLA_QK), BF16),
            jax.ShapeDtypeStruct((H, nc, MLA_V, ts), BF16),
        ],
        compiler_params=_cparams("arbitrary"),
        name="mla_proj",
    )(x, g.reshape(1, D), wa_ext, qag, kvag, wuqt, wkn, wvt, qg_col, kgn, kgr, kgrs, cos_t, sin_t, c2, s2)


def _mla_attn_body(qt_ref, k_ref, vt_ref, o_ref, *, tk, tc, shift_by_max):
    qt = qt_ref[...]
    tq = qt.shape[1]
    n_chunks = k_ref.shape[0] // tk
    per = tk // tc

    def scores(c):
        return _dot(k_ref[c * tk:(c + 1) * tk, :], qt)

    def weighted_values(c, pb):
        pv = _dot(vt_ref[c * per], pb[:tc])
        for u in range(1, per):
            pv += _dot(vt_ref[c * per + u], pb[u * tc:(u + 1) * tc])
        return pv

    m = jnp.full((1, tq), NEG_BIG, F32)
    l = jnp.zeros((1, tq), F32)
    acc = jnp.zeros((MLA_V, tq), F32)
    s_next = scores(0)
    pending = None
    for c in range(n_chunks + 1):
        s = s_next
        if c + 1 < n_chunks:
            s_next = scores(c + 1)
        if pending is not None:
            alpha, pb = pending
            pv = weighted_values(c - 1, pb)
            acc = acc + pv if alpha is None else alpha * acc + pv
        if c == n_chunks:
            break
        if shift_by_max:
            m_new = jnp.maximum(m, jnp.max(s, axis=0, keepdims=True))
            alpha = jnp.exp2(m - m_new)
            p = jnp.exp2(s - m_new)
            l = alpha * l + jnp.sum(p, axis=0, keepdims=True)
            m = m_new
        else:
            alpha = None
            p = jnp.exp2(s)
            l = l + jnp.sum(p, axis=0, keepdims=True)
        pending = (alpha, p.astype(BF16))
    o_ref[...] = (acc / l).T.astype(BF16)


def _mla_attn(qt, k, vt, shift_by_max):
    H, S, _ = k.shape
    nc, tc = vt.shape[1], vt.shape[3]
    tq = min(ATT_TQ, S)
    tk = min(ATT_TK, S)
    return pl.pallas_call(
        functools.partial(_mla_attn_body, tk=tk, tc=tc, shift_by_max=shift_by_max),
        grid=(H, S // tq),
        in_specs=[
            pl.BlockSpec((MLA_QK, tq), lambda h, i: (h, i)),
            pl.BlockSpec((None, S, MLA_QK), lambda h, i: (h, 0, 0)),
            pl.BlockSpec((None, nc, MLA_V, tc), lambda h, i: (h, 0, 0, 0)),
        ],
        out_specs=pl.BlockSpec((tq, MLA_V), lambda h, i: (i, h)),
        out_shape=jax.ShapeDtypeStruct((S, H * MLA_V), BF16),
        compiler_params=_cparams("arbitrary", "arbitrary"),
        name="mla_attn",
    )(qt, k, vt)


def _oproj_body(x_ref, o_ref, w_ref, y_ref):
    y_ref[...] = x_ref[...] + _dot(o_ref[...], w_ref[...])


def _oproj(x, o, w, layer):
    S, D = x.shape
    ts = min(OPROJ_TS, S)
    return pl.pallas_call(
        _oproj_body,
        grid=(S // ts,),
        in_specs=[
            pl.BlockSpec((ts, D), lambda i: (i, 0)),
            pl.BlockSpec((ts, o.shape[1]), lambda i: (i, 0)),
            _stacked(w, layer),
        ],
        out_specs=pl.BlockSpec((ts, D), lambda i: (i, 0)),
        out_shape=jax.ShapeDtypeStruct((S, D), F32),
        compiler_params=_cparams("arbitrary"),
        name="oproj",
    )(x, o, w)


def _swa_proj_body(x_ref, g_ref, wt_ref, gcol_ref, qt_ref, kt_ref, vt_ref):
    nq = SWA_Q_HEADS * SWA_DH
    nk = SWA_KV_HEADS * SWA_DH
    nh = SWA_Q_HEADS + SWA_KV_HEADS
    h = _rms_rows(x_ref[...], g_ref[...]).astype(BF16)
    qkv = _dot_nt(wt_ref[...], h)
    ts = qkv.shape[1]
    qk = qkv[:nq + nk].reshape(nh, SWA_DH, ts)
    ms = jnp.sum(qk * qk, axis=1, keepdims=True) * (1.0 / SWA_DH)
    y = (qk * lax.rsqrt(ms + EPS)).reshape(nq + nk, ts) * gcol_ref[...]
    qt_ref[...] = y[:nq].astype(BF16)
    kt_ref[...] = y[nq:].astype(BF16)
    vt_ref[...] = qkv[nq + nk:].astype(BF16)


def _swa_proj(x, g, wt, gcol, layer):
    S, D = x.shape
    nq = SWA_Q_HEADS * SWA_DH
    nk = SWA_KV_HEADS * SWA_DH
    ts = min(PROJ_TS, S)
    const = lambda i: (0, 0)
    return pl.pallas_call(
        _swa_proj_body,
        grid=(S // ts,),
        in_specs=[
            pl.BlockSpec((ts, D), lambda i: (i, 0)),
            pl.BlockSpec((1, D), const),
            _stacked(wt, layer),
            pl.BlockSpec((nq + nk, 1), const),
        ],
        out_specs=[
            pl.BlockSpec((nq, ts), lambda i: (0, i)),
            pl.BlockSpec((nk, ts), lambda i: (0, i)),
            pl.BlockSpec((nk, ts), lambda i: (0, i)),
        ],
        out_shape=[
            jax.ShapeDtypeStruct((nq, S), BF16),
            jax.ShapeDtypeStruct((nk, S), BF16),
            jax.ShapeDtypeStruct((nk, S), BF16),
        ],
        compiler_params=_cparams("arbitrary"),
        name="swa_proj",
    )(x, g.reshape(1, D), wt, gcol)


def _alibi_slopes2():
    n = SWA_Q_HEADS
    s = (2.0 ** (-8.0 * np.arange(1, n + 1) / n)).astype(np.float32)
    return [float(v) * LOG2E for v in s]


def _swa_attn_body(qt_ref, ktp_ref, ktc_ref, ktn_ref, vtp_ref, vtc_ref, vtn_ref,
                   pqr_ref, pkp_ref, pkc_ref, pkn_ref, sink_ref, o_ref, *, seq_len, shift_by_max):
    W = WINDOW
    G = SWA_GROUP
    dh = SWA_DH
    tq = qt_ref.shape[1]
    nsub = tq // W
    t0 = pl.program_id(0) * tq
    slopes2 = _alibi_slopes2()

    kt_all = jnp.concatenate([ktp_ref[...], ktc_ref[...], ktn_ref[...]], axis=1)
    vt_all = jnp.concatenate([vtp_ref[...], vtc_ref[...], vtn_ref[...]], axis=1)
    pk_all = jnp.concatenate([pkp_ref[...], pkc_ref[...], pkn_ref[...]], axis=0)
    pq_all = pqr_ref[...]

    for j in range(nsub):
        kidx = t0 + W * (j - 1) + lax.broadcasted_iota(jnp.int32, (3 * W, W), 0)
        qidx = t0 + W * j + lax.broadcasted_iota(jnp.int32, (3 * W, W), 1)
        rel = kidx - qidx
        ok = (rel >= -W) & (rel <= W) & (kidx >= 0) & (kidx < seq_len)
        dist = jnp.abs(pk_all[W * j:W * (j + 3)] - pq_all[:, W * j:W * (j + 1)])
        dist = jnp.where(ok, dist, -NEG_BIG)
        for u in range(SWA_KV_HEADS):
            ktu = kt_all[u * dh:(u + 1) * dh, W * j:W * (j + 3)]
            vtu = vt_all[u * dh:(u + 1) * dh, W * j:W * (j + 3)]
            q4 = jnp.concatenate(
                [qt_ref[(u * G + g) * dh:(u * G + g + 1) * dh, W * j:W * (j + 1)] for g in range(G)],
                axis=1)
            s = _dot_tn(ktu, q4)
            s = jnp.concatenate(
                [s[:, g * W:(g + 1) * W] - dist * slopes2[u * G + g] for g in range(G)], axis=1)
            snk = sink_ref[u]
            if shift_by_max:
                m = jnp.maximum(jnp.max(s, axis=0, keepdims=True), snk)
                p = jnp.exp2(s - m)
                den = jnp.sum(p, axis=0, keepdims=True) + jnp.exp2(snk - m)
            else:
                p = jnp.exp2(s)
                den = jnp.sum(p, axis=0, keepdims=True) + jnp.exp2(snk)
            o = _dot(vtu, p.astype(BF16)) / den
            for pr in range(G // 2):
                two = jnp.concatenate(
                    [o[:, (2 * pr) * W:(2 * pr + 1) * W], o[:, (2 * pr + 1) * W:(2 * pr + 2) * W]], axis=0)
                col = (u * G + 2 * pr) * dh
                o_ref[W * j:W * (j + 1), col:col + 2 * dh] = two.T.astype(BF16)


def _swa_attn(qt, kt, vt, pos_row_f, pos_col_f, sink2, shift_by_max):
    nq, S = qt.shape
    nk = kt.shape[0]
    W = WINDOW
    tq = min(SWA_TQ, S)
    nsub = tq // W
    nblk = S // W
    cur = lambda i: (0, i)
    prev = lambda i: (0, jnp.maximum(i * nsub - 1, 0))
    nxt = lambda i: (0, jnp.minimum((i + 1) * nsub, nblk - 1))
    cur_c = lambda i: (i, 0)
    prev_c = lambda i: (jnp.maximum(i * nsub - 1, 0), 0)
    nxt_c = lambda i: (jnp.minimum((i + 1) * nsub, nblk - 1), 0)
    return pl.pallas_call(
        functools.partial(_swa_attn_body, seq_len=S, shift_by_max=shift_by_max),
        grid=(S // tq,),
        in_specs=[
            pl.BlockSpec((nq, tq), cur),
            pl.BlockSpec((nk, W), prev), pl.BlockSpec((nk, tq), cur), pl.BlockSpec((nk, W), nxt),
            pl.BlockSpec((nk, W), prev), pl.BlockSpec((nk, tq), cur), pl.BlockSpec((nk, W), nxt),
            pl.BlockSpec((1, tq), cur),
            pl.BlockSpec((W, 1), prev_c), pl.BlockSpec((tq, 1), cur_c), pl.BlockSpec((W, 1), nxt_c),
            pl.BlockSpec(sink2.shape, lambda i: (0, 0, 0)),
        ],
        out_specs=pl.BlockSpec((tq, nq), lambda i: (i, 0)),
        out_shape=jax.ShapeDtypeStruct((S, nq), BF16),
        compiler_params=_cparams("arbitrary"),
        name="swa_attn",
    )(qt, kt, kt, kt, vt, vt, vt, pos_row_f, pos_col_f, pos_col_f, pos_col_f, sink2)


def kernel(x, positions, norm_g, ffn_w_gate, ffn_w_up, ffn_w_down, mla_w_a, mla_q_a_g, mla_kv_a_g,
           mla_w_uq, mla_w_ukv, mla_q_g, mla_k_g, mla_w_o, swa_w_qkv, swa_q_g, swa_k_g, swa_sink, swa_w_o):
    B, S, D = x.shape
    assert B == 1
    depth = norm_g.shape[0]
    H = MLA_HEADS
    half = MLA_ROPE // 2

    pos_row = positions.reshape(1, S)
    pos_col = positions.reshape(S, 1)
    cos_t, sin_t, c2, s2 = _rope_tables(pos_row, pos_col)
    pos_row_f = pos_row.astype(F32)
    pos_col_f = pos_col.astype(F32)

    wg, wu, wd = ffn_w_gate, ffn_w_up, ffn_w_down

    n_mla = mla_w_a.shape[0]
    pe0 = 2 * MLA_RANK
    wa_ext = jnp.concatenate(
        [mla_w_a, mla_w_a[:, :, pe0 + half:pe0 + MLA_ROPE], mla_w_a[:, :, pe0:pe0 + half]], axis=2).astype(BF16)
    wuqt = jnp.swapaxes(mla_w_uq, 1, 2).astype(BF16)
    wukv = mla_w_ukv.reshape(n_mla, MLA_RANK, H, MLA_NOPE + MLA_V)
    wkn = wukv[:, :, :, :MLA_NOPE].reshape(n_mla, MLA_RANK, H * MLA_NOPE).astype(BF16)
    wvt = jnp.swapaxes(wukv[:, :, :, MLA_NOPE:].reshape(n_mla, MLA_RANK, H * MLA_V), 1, 2).astype(BF16)
    mla_wo = mla_w_o.astype(BF16)
    swa_wt = jnp.swapaxes(swa_w_qkv, 1, 2).astype(BF16)
    swa_wo = swa_w_o.astype(BF16)

    xs = x.reshape(S, D)
    for i in range(depth):
        g = norm_g[i]
        xs = _ffn(xs, g[0], wg, wu, wd, i, 0)
        j = i // 2
        if i % 2 == 0:
            qscale = LOG2E / math.sqrt(MLA_QK)
            qg_col = (mla_q_g[j] * qscale).reshape(MLA_QK, 1)
            kg = mla_k_g[j]
            kgn = kg[:MLA_NOPE].reshape(1, MLA_NOPE)
            kgr = kg[MLA_NOPE:].reshape(1, MLA_ROPE)
            kgrs = jnp.concatenate([kg[MLA_NOPE + half:], kg[MLA_NOPE:MLA_NOPE + half]]).reshape(1, MLA_ROPE)
            qt, k, vt = _mla_proj(xs, g[1], wa_ext, mla_q_a_g[j].reshape(1, MLA_RANK),
                                  mla_kv_a_g[j].reshape(1, MLA_RANK), wuqt, wkn, wvt,
                                  qg_col, kgn, kgr, kgrs, cos_t, sin_t, c2, s2, j)
            score_bound = (1.02 * LOG2E * math.sqrt(MLA_QK)
                           * jnp.max(jnp.abs(mla_q_g[j])) * jnp.max(jnp.abs(mla_k_g[j])))
            o = lax.cond(score_bound <= UNSHIFTED_EXP2_LIMIT,
                         lambda: _mla_attn(qt, k, vt, False),
                         lambda: _mla_attn(qt, k, vt, True))
            xs = _oproj(xs, o, mla_wo, j)
        else:
            qscale = LOG2E / math.sqrt(SWA_DH)
            gcol = jnp.concatenate([jnp.tile(swa_q_g[j] * qscale, SWA_Q_HEADS),
                                    jnp.tile(swa_k_g[j], SWA_KV_HEADS)]).reshape(-1, 1)
            qt, kt, vt = _swa_proj(xs, g[1], swa_wt, gcol, j)
            sink2 = jnp.repeat(swa_sink[j] * LOG2E, WINDOW).reshape(SWA_KV_HEADS, 1, SWA_GROUP * WINDOW)
            score_bound = jnp.maximum(
                1.02 * LOG2E * math.sqrt(SWA_DH) * jnp.max(jnp.abs(swa_q_g[j])) * jnp.max(jnp.abs(swa_k_g[j])),
                jnp.max(jnp.abs(sink2)))
            o = lax.cond(score_bound <= UNSHIFTED_EXP2_LIMIT,
                         lambda: _swa_attn(qt, kt, vt, pos_row_f, pos_col_f, sink2, False),
                         lambda: _swa_attn(qt, kt, vt, pos_row_f, pos_col_f, sink2, True))
            xs = _oproj(xs, o, swa_wo, j)
        xs = _ffn(xs, g[2], wg, wu, wd, i, 1)
    return xs.reshape(B, S, D)
```
